```python
import math
import jax, jax.numpy as jnp
from jax import lax
import numpy as np

D_MODEL = 2048
BATCH = 4
SEQ = 2048
DEPTH = 1
DEC_BATCH = 32
DEC_SEQ = 1
PAST_LEN = 16384
PAGE_SIZE = 128

N_HEADS = 8
HEAD_DIM = 64
ATTN_W = N_HEADS * 2 * HEAD_DIM
LAMBDA_INIT = 0.8 - 0.6 * math.exp(-0.3 * (1 - 1))
SCALE = HEAD_DIM ** -0.5
Q_BLOCK = 128
CONV_W = D_MODEL - ATTN_W
CONV_WIDTH = 31
CONV_STATE = CONV_WIDTH - 1
N_KEYS = 128
N_EXPERTS = N_KEYS * N_KEYS
P_HEADS = 8
P_KEY_HALF = 128
P_TOPK = 16
TOK_BLOCK = 128
NEG = -1e30
EPS = 1e-6

kernel_name = "hymba_diffattn_conformer_peer_step"


def rms_norm(x, g):
    xf = x.astype(jnp.float32)
    y = xf * lax.rsqrt(jnp.mean(xf * xf, axis=-1, keepdims=True) + EPS)
    return (y * g.astype(jnp.float32)).astype(x.dtype)


def diff_lambda(lq1, lk1, lq2, lk2):
    f = lambda a: a.astype(jnp.float32)
    return jnp.exp(jnp.sum(f(lq1) * f(lk1))) - jnp.exp(jnp.sum(f(lq2) * f(lk2))) + LAMBDA_INIT


def mixer_inputs(x, norm_mix_g, w_in, q_norm_g, k_norm_g):
    B, S, _ = x.shape
    xn = rms_norm(x, norm_mix_g)
    h = jnp.einsum('bsd,de->bse', xn, w_in)
    q, k, v, a, gt = jnp.split(h, [ATTN_W, 2 * ATTN_W, 3 * ATTN_W, 3 * ATTN_W + CONV_W], axis=-1)
    q = rms_norm(q.reshape(B, S, N_HEADS, 2, HEAD_DIM), q_norm_g)
    k = rms_norm(k.reshape(B, S, N_HEADS, 2, HEAD_DIM), k_norm_g)
    v = v.reshape(B, S, N_HEADS, 2 * HEAD_DIM)
    u = a * jax.nn.sigmoid(gt)
    return q, k, v, u


def diff_attn_prompt(q, k, v, lam):
    B, S, H, _, Dh = q.shape
    nb = S // Q_BLOCK
    qb = q.reshape(B, nb, Q_BLOCK, H, 2, Dh).transpose(1, 0, 2, 3, 4, 5)
    kpos = jnp.arange(S)
    vf = v.astype(jnp.float32)

    def one(args):
        qblk, bi = args
        s = jnp.einsum('bqhcd,bkhcd->bhcqk', qblk, k).astype(jnp.float32) * SCALE
        qpos = bi * Q_BLOCK + jnp.arange(Q_BLOCK)
        s = jnp.where(kpos[None, :] <= qpos[:, None], s, NEG)
        p = jax.nn.softmax(s, axis=-1)
        w = p[:, :, 0] - lam * p[:, :, 1]
        return jnp.einsum('bhqk,bkhe->bqhe', w, vf)

    o = lax.map(one, (qb, jnp.arange(nb)))
    return o.transpose(1, 0, 2, 3, 4).reshape(B, S, H, 2 * Dh)


def diff_attn_sample(q, k_new, v_new, cache_k, cache_v, page_table, lam):
    Bd, Sq, H, _, Dh = q.shape
    qf = q.astype(jnp.float32)

    def update(carry, kb, vb, mask):
        m, l, acc = carry
        s = jnp.einsum('bqhcd,bkhcd->bhcqk', qf, kb.astype(jnp.float32)) * SCALE
        if mask is not None:
            s = jnp.where(mask, s, NEG)
        m_new = jnp.maximum(m, jnp.max(s, axis=-1))
        corr = jnp.exp(m - m_new)
        p = jnp.exp(s - m_new[..., None])
        l = l * corr + jnp.sum(p, axis=-1)
        acc = acc * corr[..., None] + jnp.einsum('bhcqk,bkhe->bhcqe', p, vb.astype(jnp.float32))
        return (m_new, l, acc)

    def page_step(carry, pt):
        kb = cache_k[pt].reshape(Bd, PAGE_SIZE, H, 2, Dh)
        vb = cache_v[pt]
        return update(carry, kb, vb, None), None

    init = (jnp.full((Bd, H, 2, Sq), NEG, jnp.float32),
            jnp.zeros((Bd, H, 2, Sq), jnp.float32),
            jnp.zeros((Bd, H, 2, Sq, 2 * Dh), jnp.float32))
    carry, _ = lax.scan(page_step, init, page_table.T)
    mask = jnp.tril(jnp.ones((Sq, Sq), dtype=bool))
    m, l, acc = update(carry, k_new, v_new, mask)
    o = acc / l[..., None]
    w = o[:, :, 0] - lam * o[:, :, 1]
    return w.transpose(0, 2, 1, 3)


def causal_dwconv(u_ext, w_dw, b_dw):
    C = u_ext.shape[-1]
    y = lax.conv_general_dilated(u_ext, w_dw[:, None, :].astype(u_ext.dtype), window_strides=(1,),
                                 padding='VALID', dimension_numbers=('NWC', 'WIO', 'NWC'),
                                 feature_group_count=C)
    return y + b_dw


def peer(xn, w_query, sub_keys, expert_u, expert_v):
    B, S, D = xn.shape
    T = B * S
    xt = xn.reshape(T, D)
    q = jnp.einsum('td,dk->tk', xt, w_query).reshape(T, P_HEADS, 2, P_KEY_HALF)
    s = jnp.einsum('thcd,hcnd->thcn', q, sub_keys).astype(jnp.float32)
    s1, i1 = lax.top_k(s[:, :, 0], P_TOPK)
    s2, i2 = lax.top_k(s[:, :, 1], P_TOPK)
    cand = (s1[..., :, None] + s2[..., None, :]).reshape(T, P_HEADS, P_TOPK * P_TOPK)
    cand_id = (i1[..., :, None] * N_KEYS + i2[..., None, :]).reshape(T, P_HEADS, P_TOPK * P_TOPK)
    top_s, pos = lax.top_k(cand, P_TOPK)
    ids = jnp.take_along_axis(cand_id, pos, axis=-1).reshape(T, P_HEADS * P_TOPK)
    g = jax.nn.softmax(top_s, axis=-1).reshape(T, P_HEADS * P_TOPK)
    nb = -(-T // TOK_BLOCK)
    pad = nb * TOK_BLOCK - T
    xp = jnp.pad(xt, ((0, pad), (0, 0))).reshape(nb, TOK_BLOCK, D)
    ip = jnp.pad(ids, ((0, pad), (0, 0))).reshape(nb, TOK_BLOCK, -1)
    gp = jnp.pad(g, ((0, pad), (0, 0))).reshape(nb, TOK_BLOCK, -1)

    def blk(args):
        xb, ib, gb = args
        h = jnp.einsum('td,ted->te', xb, expert_u[ib]).astype(jnp.float32)
        a = (jax.nn.gelu(h, approximate=False) * gb).astype(xb.dtype)
        return jnp.einsum('te,ted->td', a, expert_v[ib])

    out = lax.map(blk, (xp, ip, gp)).reshape(nb * TOK_BLOCK, D)[:T]
    return out.reshape(B, S, D)


def finish(x, attn_o, conv_o, subln_g, conv_norm_g, w_out, norm_ffn_g,
           w_query, sub_keys, expert_u, expert_v):
    B, S, _ = x.shape
    a = (rms_norm(attn_o, subln_g) * (1.0 - LAMBDA_INIT)).astype(x.dtype).reshape(B, S, ATTN_W)
    c = jax.nn.silu(rms_norm(conv_o, conv_norm_g))
    x1 = x + jnp.einsum('bse,ed->bsd', jnp.concatenate([a, c], axis=-1), w_out)
    return x1 + peer(rms_norm(x1, norm_ffn_g), w_query, sub_keys, expert_u, expert_v)


def setup_inputs(seed: int = 0) -> dict:
    key = jax.random.key(seed)
    ks = jax.random.split(key, 24)
    n_pages = PAST_LEN // PAGE_SIZE
    n_pool = (DEC_BATCH * n_pages * 5) // 4
    nrm = lambda k, shape, s: jax.random.normal(k, shape, jnp.float32) * s
    perm = jax.random.permutation(ks[0], n_pool)[:DEC_BATCH * n_pages]
    return {
        "x_prompt": nrm(ks[1], (BATCH, SEQ, D_MODEL), 1.0),
        "x_sample": nrm(ks[2], (DEC_BATCH, DEC_SEQ, D_MODEL), 1.0),
        "cache_k": nrm(ks[3], (n_pool, PAGE_SIZE, N_HEADS, 2 * HEAD_DIM), 1.0),
        "cache_v": nrm(ks[4], (n_pool, PAGE_SIZE, N_HEADS, 2 * HEAD_DIM), 1.0),
        "state_conv": nrm(ks[5], (DEC_BATCH, CONV_STATE, CONV_W), 0.5),
        "page_table": perm.reshape(DEC_BATCH, n_pages).astype(jnp.int32),
        "norm_mix_g": 1.0 + nrm(ks[6], (D_MODEL,), 0.02),
        "w_in": nrm(ks[7], (D_MODEL, 3 * ATTN_W + 2 * CONV_W), D_MODEL ** -0.5),
        "q_norm_g": 1.0 + nrm(ks[8], (HEAD_DIM,), 0.02),
        "k_norm_g": 1.0 + nrm(ks[9], (HEAD_DIM,), 0.02),
        "lambda_q1": nrm(ks[10], (HEAD_DIM,), 0.1),
        "lambda_k1": nrm(ks[11], (HEAD_DIM,), 0.1),
        "lambda_q2": nrm(ks[12], (HEAD_DIM,), 0.1),
        "lambda_k2": nrm(ks[13], (HEAD_DIM,), 0.1),
        "subln_g": 1.0 + nrm(ks[14], (2 * HEAD_DIM,), 0.02),
        "w_dw": nrm(ks[15], (CONV_WIDTH, CONV_W), CONV_WIDTH ** -0.5),
        "b_dw": nrm(ks[16], (CONV_W,), 0.02),
        "conv_norm_g": 1.0 + nrm(ks[17], (CONV_W,), 0.02),
        "w_out": nrm(ks[18], (D_MODEL, D_MODEL), D_MODEL ** -0.5),
        "norm_ffn_g": 1.0 + nrm(ks[19], (D_MODEL,), 0.02),
        "w_query": nrm(ks[20], (D_MODEL, P_HEADS * 2 * P_KEY_HALF), D_MODEL ** -0.5),
        "sub_keys": nrm(ks[21], (P_HEADS, 2, N_KEYS, P_KEY_HALF), P_KEY_HALF ** -0.5),
        "expert_u": nrm(ks[22], (N_EXPERTS, D_MODEL), D_MODEL ** -0.5),
        "expert_v": nrm(ks[23], (N_EXPERTS, D_MODEL), 0.1),
    }


def reference(x_prompt, x_sample, cache_k, cache_v, state_conv, page_table,
              norm_mix_g, w_in, q_norm_g, k_norm_g, lambda_q1, lambda_k1, lambda_q2, lambda_k2,
              subln_g, w_dw, b_dw, conv_norm_g, w_out, norm_ffn_g,
              w_query, sub_keys, expert_u, expert_v):
    lam = diff_lambda(lambda_q1, lambda_k1, lambda_q2, lambda_k2)
    y_prompt, y_sample = x_prompt, x_sample
    for _layer in range(DEPTH):
        B, S, _ = y_prompt.shape
        q, k, v, u = mixer_inputs(y_prompt, norm_mix_g, w_in, q_norm_g, k_norm_g)
        attn_o = diff_attn_prompt(q, k, v, lam)
        u_ext = jnp.concatenate([jnp.zeros((B, CONV_STATE, CONV_W), u.dtype), u], axis=1)
        conv_o = causal_dwconv(u_ext, w_dw, b_dw)
        k_prompt = k.reshape(B, S, N_HEADS, 2 * HEAD_DIM)
        v_prompt = v
        conv_prompt = u_ext[:, -CONV_STATE:]
        y_prompt = finish(y_prompt, attn_o, conv_o, subln_g, conv_norm_g, w_out, norm_ffn_g,
                          w_query, sub_keys, expert_u, expert_v)
        Bd, Sd, _ = y_sample.shape
        qs, ks_, vs, us = mixer_inputs(y_sample, norm_mix_g, w_in, q_norm_g, k_norm_g)
        attn_s = diff_attn_sample(qs, ks_, vs, cache_k, cache_v, page_table, lam)
        us_ext = jnp.concatenate([state_conv.astype(us.dtype), us], axis=1)
        conv_s = causal_dwconv(us_ext, w_dw, b_dw)
        k_sample = ks_.reshape(Bd, Sd, N_HEADS, 2 * HEAD_DIM)
        v_sample = vs
        conv_sample = us_ext[:, -CONV_STATE:]
        y_sample = finish(y_sample, attn_s, conv_s, subln_g, conv_norm_g, w_out, norm_ffn_g,
                          w_query, sub_keys, expert_u, expert_v)
    return (y_prompt, y_sample, k_prompt, v_prompt, conv_prompt, k_sample, v_sample, conv_sample)
```

```python
import functools
import math

import jax
import jax.numpy as jnp
from jax import lax
from jax.experimental import pallas as pl
from jax.experimental.pallas import tpu as pltpu

F32 = jnp.float32
BF16 = jnp.bfloat16

D_MODEL = 2048
N_HEADS = 8
HEAD_DIM = 64
HEAD_W = 2 * HEAD_DIM
ATTN_W = N_HEADS * HEAD_W
CONV_W = D_MODEL - ATTN_W
CONV_WIDTH = 31
CONV_STATE = CONV_WIDTH - 1
LAMBDA_INIT = 0.8 - 0.6 * math.exp(-0.3 * (1 - 1))
SCALE = HEAD_DIM ** -0.5
PAGE_SIZE = 128
N_KEYS = 128
P_HEADS = 8
P_TOPK = 16
EPS = 1e-6
NEG = -1e30

SUBLANES = 8
LANES = 128
VMEM_LIMIT_BYTES = 56 * 1024 * 1024

_NT = (((1,), (1,)), ((), ()))
_TN = (((0,), (0,)), ((), ()))


def _params(*sem):
    return pltpu.CompilerParams(dimension_semantics=sem, vmem_limit_bytes=VMEM_LIMIT_BYTES)


def _resident(shape):
    return pl.BlockSpec(shape, lambda *_: (0,) * len(shape), pipeline_mode=pl.Buffered(1))


def _rms(x, gain):
    ms = jnp.mean(x * x, axis=-1, keepdims=True)
    return x * lax.rsqrt(ms + EPS) * gain


def _sigmoid(x):
    return 1.0 / (1.0 + jnp.exp(-x))


def _diff_lambda(lam_ref):
    l = lam_ref[...]
    e1 = jnp.exp(jnp.sum(l[0:1] * l[1:2], axis=-1, keepdims=True))
    e2 = jnp.exp(jnp.sum(l[2:3] * l[3:4], axis=-1, keepdims=True))
    return e1 - e2 + LAMBDA_INIT


def _in_proj_kernel(x_ref, g_ref, w_ref, qg_ref, kg_ref, p_ref,
                    qb_ref, k_ref, kb_ref, v_ref, vb_ref, u_ref):
    xn = _rms(x_ref[...], g_ref[...]).astype(BF16)

    def proj(col):
        return jnp.dot(xn, w_ref[:, col:col + ATTN_W], preferred_element_type=F32)

    def head_norm(y, gain):
        y2 = y * y
        hi = y2.astype(BF16)
        lo = (y2 - hi.astype(F32)).astype(BF16)
        p = p_ref[...]
        w = p.shape[0]
        parts = []
        for c in range(ATTN_W // w):
            sl = slice(c * w, (c + 1) * w)
            parts.append(jnp.dot(hi[:, sl], p, preferred_element_type=F32)
                         + jnp.dot(lo[:, sl], p, preferred_element_type=F32))
        ss = jnp.concatenate(parts, axis=-1)
        return y * lax.rsqrt(ss * (1.0 / HEAD_DIM) + EPS) * gain

    q = head_norm(proj(0), qg_ref[...])
    qb_ref[...] = (q * SCALE).astype(BF16)
    k = head_norm(proj(ATTN_W), kg_ref[...])
    k_ref[...] = k
    kb_ref[...] = k.astype(BF16)
    v = proj(2 * ATTN_W)
    v_ref[...] = v
    vb_ref[...] = v.astype(BF16)
    a = proj(3 * ATTN_W)
    gt = proj(3 * ATTN_W + CONV_W)
    u_ref[...] = a * _sigmoid(gt)


def _in_proj(x2d, norm_g, w_in_b, qg, kg, pmat, tm):
    t = x2d.shape[0]
    row = lambda i: (i, 0)
    blk = lambda: pl.BlockSpec((tm, ATTN_W), row)
    return pl.pallas_call(
        _in_proj_kernel,
        grid=(t // tm,),
        in_specs=[pl.BlockSpec((tm, D_MODEL), row),
                  _resident((1, D_MODEL)),
                  _resident(w_in_b.shape),
                  _resident((1, ATTN_W)),
                  _resident((1, ATTN_W)),
                  _resident(pmat.shape)],
        out_specs=[blk(), blk(), blk(), blk(), blk(), blk()],
        out_shape=[jax.ShapeDtypeStruct((t, ATTN_W), BF16),
                   jax.ShapeDtypeStruct((t, ATTN_W), F32),
                   jax.ShapeDtypeStruct((t, ATTN_W), BF16),
                   jax.ShapeDtypeStruct((t, ATTN_W), F32),
                   jax.ShapeDtypeStruct((t, ATTN_W), BF16),
                   jax.ShapeDtypeStruct((t, CONV_W), F32)],
        compiler_params=_params("arbitrary"),
        name="in_proj",
    )(x2d, norm_g, w_in_b, qg, kg, pmat)


def _attn_prompt_kernel(lam_ref, sg_ref, q_ref, k_ref, v_ref, o_ref, *, tq):
    i = pl.program_id(2)
    lam = _diff_lambda(lam_ref)
    q = q_ref[0]
    lane = lax.broadcasted_iota(jnp.int32, q.shape, 1)
    zero = jnp.zeros_like(q)
    q1 = jnp.where(lane < HEAD_DIM, q, zero)
    q2 = jnp.where(lane >= HEAD_DIM, q, zero)

    def update(s, m, l, acc, vj):
        m_new = jnp.maximum(m, jnp.max(s, axis=-1, keepdims=True))
        corr = jnp.exp(m - m_new)
        p = jnp.exp(s - m_new)
        l = l * corr + jnp.sum(p, axis=-1, keepdims=True)
        acc = acc * corr + jnp.dot(p.astype(BF16), vj, preferred_element_type=F32)
        return m_new, l, acc

    def step(j, carry, diagonal):
        m1, l1, a1, m2, l2, a2 = carry
        start = pl.multiple_of(j * tq, tq)
        kj = k_ref[0, pl.ds(start, tq), :]
        vj = v_ref[0, pl.ds(start, tq), :]
        s1 = lax.dot_general(q1, kj, _NT, preferred_element_type=F32)
        s2 = lax.dot_general(q2, kj, _NT, preferred_element_type=F32)
        if diagonal:
            r = lax.broadcasted_iota(jnp.int32, s1.shape, 0)
            c = lax.broadcasted_iota(jnp.int32, s1.shape, 1)
            keep = c <= r
            s1 = jnp.where(keep, s1, NEG)
            s2 = jnp.where(keep, s2, NEG)
        m1, l1, a1 = update(s1, m1, l1, a1, vj)
        m2, l2, a2 = update(s2, m2, l2, a2, vj)
        return m1, l1, a1, m2, l2, a2

    col = lambda v: jnp.full((tq, 1), v, F32)
    acc0 = jnp.zeros((tq, HEAD_W), F32)
    init = (col(NEG), col(0.0), acc0, col(NEG), col(0.0), acc0)
    carry = lax.fori_loop(0, i, lambda j, c: step(j, c, False), init)
    m1, l1, a1, m2, l2, a2 = step(i, carry, True)
    o = a1 / l1 - lam * (a2 / l2)
    o_ref[0] = (_rms(o, sg_ref[...]) * (1.0 - LAMBDA_INIT)).astype(o_ref.dtype)


def _attn_prompt(lam4, subln_g, qb, kb, vb, tq):
    b, s, _ = qb.shape
    return pl.pallas_call(
        functools.partial(_attn_prompt_kernel, tq=tq),
        grid=(b, N_HEADS, s // tq),
        in_specs=[_resident(lam4.shape),
                  _resident((1, HEAD_W)),
                  pl.BlockSpec((1, tq, HEAD_W), lambda bi, h, i: (bi, i, h)),
                  pl.BlockSpec((1, s, HEAD_W), lambda bi, h, i: (bi, 0, h)),
                  pl.BlockSpec((1, s, HEAD_W), lambda bi, h, i: (bi, 0, h))],
        out_specs=pl.BlockSpec((1, tq, HEAD_W), lambda bi, h, i: (bi, i, h)),
        out_shape=jax.ShapeDtypeStruct((b, s, ATTN_W), BF16),
        compiler_params=_params("arbitrary", "arbitrary", "arbitrary"),
        name="attn_prompt",
    )(lam4, subln_g, qb, kb, vb)


def _attn_sample_kernel(pt_ref, lam_ref, sg_ref, seg_ref, q_ref, kn_ref, vn_ref, *rest, pages):
    k_refs = rest[:pages]
    v_refs = rest[pages:2 * pages]
    o_ref = rest[2 * pages]
    m_ref, l_ref, acc_ref = rest[2 * pages + 1:]
    p = pl.program_id(1)
    q = q_ref[0]
    seg = seg_ref[...]

    def scores(kblk):
        n = kblk.shape[0]
        prod = (kblk * q[None]).reshape(n * N_HEADS, HEAD_W).astype(BF16)
        s = jnp.dot(prod, seg, preferred_element_type=F32)
        return s.reshape(n, N_HEADS, 2 * HEAD_W)

    @pl.when(p == 0)
    def _():
        m_ref[...] = scores(kn_ref[...])[0]
        l_ref[...] = jnp.ones_like(l_ref)
        vn = vn_ref[0]
        acc_ref[...] = jnp.concatenate([vn, vn], axis=-1)

    for r in range(pages):
        s = scores(k_refs[r][0])
        v = v_refs[r][0]
        m_old = m_ref[...]
        m_new = jnp.maximum(m_old, jnp.max(s, axis=0))
        corr = jnp.exp(m_old - m_new)
        pr = jnp.exp(s - m_new[None])
        l_ref[...] = l_ref[...] * corr + jnp.sum(pr, axis=0)
        pv = jnp.concatenate([jnp.sum(pr[..., :HEAD_W] * v, axis=0),
                              jnp.sum(pr[..., HEAD_W:] * v, axis=0)], axis=-1)
        acc_ref[...] = acc_ref[...] * corr + pv
        m_ref[...] = m_new

    @pl.when(p == pl.num_programs(1) - 1)
    def _():
        o = acc_ref[...] / l_ref[...]
        w = o[:, :HEAD_W] - _diff_lambda(lam_ref) * o[:, HEAD_W:]
        o_ref[0] = (_rms(w, sg_ref[...]) * (1.0 - LAMBDA_INIT)).astype(o_ref.dtype)


def _attn_sample(page_table, lam4, subln_g, seg, q, k_new, v_new, cache_k, cache_v, pages):
    bd, n_pages = page_table.shape
    page_blk = (1, PAGE_SIZE, N_HEADS, HEAD_W)

    def page_spec(r):
        return pl.BlockSpec(page_blk, lambda b, p, pt: (pt[b, p * pages + r], 0, 0, 0))

    tok = pl.BlockSpec((1, N_HEADS, HEAD_W), lambda b, p, pt: (b, 0, 0))
    const = lambda shape: pl.BlockSpec(shape, lambda b, p, pt: (0,) * len(shape))
    grid_spec = pltpu.PrefetchScalarGridSpec(
        num_scalar_prefetch=1,
        grid=(bd, n_pages // pages),
        in_specs=[const(lam4.shape), const((1, HEAD_W)), const(seg.shape), tok, tok, tok]
                 + [page_spec(r) for r in range(pages)]
                 + [page_spec(r) for r in range(pages)],
        out_specs=tok,
        scratch_shapes=[pltpu.VMEM((N_HEADS, 2 * HEAD_W), F32),
                        pltpu.VMEM((N_HEADS, 2 * HEAD_W), F32),
                        pltpu.VMEM((N_HEADS, 2 * HEAD_W), F32)],
    )
    return pl.pallas_call(
        functools.partial(_attn_sample_kernel, pages=pages),
        grid_spec=grid_spec,
        out_shape=jax.ShapeDtypeStruct((bd, N_HEADS, HEAD_W), BF16),
        compiler_params=_params("arbitrary", "arbitrary"),
        name="attn_sample",
    )(page_table, lam4, subln_g, seg, q, k_new, v_new,
      *([cache_k] * pages), *([cache_v] * pages))


CONV_HALO = 32
CONV_ROWS = 64


def _conv_finish(y, g_ref):
    c = _rms(y, g_ref[...])
    return (c * _sigmoid(c)).astype(BF16)


def _conv_prompt_kernel(u_ref, halo_ref, w_ref, b_ref, g_ref, o_ref, win_ref, y_ref, *, ts):
    i = pl.program_id(1)
    halo = halo_ref[0]
    win_ref[0:CONV_HALO, :] = jnp.where(i == 0, jnp.zeros_like(halo), halo)
    win_ref[CONV_HALO:, :] = u_ref[0]
    shift = CONV_HALO - CONV_STATE

    def lane_chunk(c, _):
        lanes = pl.ds(pl.multiple_of(c * LANES, LANES), LANES)
        bias = b_ref[:, lanes]
        for r in range(ts // CONV_ROWS):
            acc = jnp.zeros((CONV_ROWS, LANES), F32) + bias
            for j in range(CONV_WIDTH):
                row0 = r * CONV_ROWS + j + shift
                acc = acc + w_ref[j:j + 1, lanes] * win_ref[row0:row0 + CONV_ROWS, lanes]
            y_ref[r * CONV_ROWS:(r + 1) * CONV_ROWS, lanes] = acc
        return 0

    lax.fori_loop(0, CONV_W // LANES, lane_chunk, 0)
    o_ref[0] = _conv_finish(y_ref[...], g_ref)


def _conv_prompt(u, w_dw, b_dw, g, ts):
    b, s, _ = u.shape
    per = ts // CONV_HALO
    return pl.pallas_call(
        functools.partial(_conv_prompt_kernel, ts=ts),
        grid=(b, s // ts),
        in_specs=[pl.BlockSpec((1, ts, CONV_W), lambda bi, i: (bi, i, 0)),
                  pl.BlockSpec((1, CONV_HALO, CONV_W),
                               lambda bi, i: (bi, jnp.maximum(i * per - 1, 0), 0)),
                  _resident(w_dw.shape),
                  _resident((1, CONV_W)),
                  _resident((1, CONV_W))],
        out_specs=pl.BlockSpec((1, ts, CONV_W), lambda bi, i: (bi, i, 0)),
        out_shape=jax.ShapeDtypeStruct((b, s, CONV_W), BF16),
        scratch_shapes=[pltpu.VMEM((CONV_HALO + ts, CONV_W), F32),
                        pltpu.VMEM((ts, CONV_W), F32)],
        compiler_params=_params("arbitrary", "arbitrary"),
        name="conv_prompt",
    )(u, u, w_dw, b_dw, g)


def _conv_sample_kernel(st_ref, u_ref, w_ref, b_ref, g_ref, o_ref):
    acc = b_ref[...] + w_ref[CONV_STATE:CONV_WIDTH, :] * u_ref[...]
    for j in range(CONV_STATE):
        acc = acc + w_ref[j:j + 1, :] * st_ref[:, j, :]
    o_ref[...] = _conv_finish(acc, g_ref)


def _conv_sample(state, u, w_dw, b_dw, g):
    bd = u.shape[0]
    return pl.pallas_call(
        _conv_sample_kernel,
        out_shape=jax.ShapeDtypeStruct((bd, CONV_W), BF16),
        compiler_params=pltpu.CompilerParams(vmem_limit_bytes=VMEM_LIMIT_BYTES),
        name="conv_sample",
    )(state, u, w_dw, b_dw, g)


def _out_proj_kernel(x_ref, a_ref, c_ref, wo_ref, g_ref, wq_ref, x1_ref, xn_ref, qp_ref):
    x1 = (x_ref[...]
          + jnp.dot(a_ref[...], wo_ref[0:ATTN_W, :], preferred_element_type=F32)
          + jnp.dot(c_ref[...], wo_ref[ATTN_W:D_MODEL, :], preferred_element_type=F32))
    x1_ref[...] = x1
    xn = _rms(x1, g_ref[...]).astype(BF16)
    xn_ref[...] = xn
    qp_ref[...] = jnp.dot(xn, wq_ref[...], preferred_element_type=F32).astype(BF16)


def _out_proj(x2d, a, c, w_out_b, g, w_query_b, tm):
    t = x2d.shape[0]
    row = lambda i: (i, 0)
    return pl.pallas_call(
        _out_proj_kernel,
        grid=(t // tm,),
        in_specs=[pl.BlockSpec((tm, D_MODEL), row),
                  pl.BlockSpec((tm, ATTN_W), row),
                  pl.BlockSpec((tm, CONV_W), row),
                  _resident(w_out_b.shape),
                  _resident((1, D_MODEL)),
                  _resident(w_query_b.shape)],
        out_specs=[pl.BlockSpec((tm, D_MODEL), row)] * 3,
        out_shape=[jax.ShapeDtypeStruct((t, D_MODEL), F32),
                   jax.ShapeDtypeStruct((t, D_MODEL), BF16),
                   jax.ShapeDtypeStruct((t, D_MODEL), BF16)],
        compiler_params=_params("arbitrary"),
        name="out_proj",
    )(x2d, a, c, w_out_b, g, w_query_b)


_PAIRS = [(a, b) for a in range(P_TOPK) for b in range(P_TOPK) if (a + 1) * (b + 1) <= P_TOPK]


def _stable_topk(x):
    rows = x.shape[0]
    row = lax.broadcasted_iota(jnp.int32, x.shape, 0).astype(F32)
    rank = jnp.full(x.shape, float(P_TOPK), F32)
    vals = []
    for r in range(P_TOPK):
        m = jnp.max(x, axis=0, keepdims=True)
        first = jnp.min(jnp.where(x == m, row, float(rows)), axis=0, keepdims=True)
        hit = row == first
        rank = jnp.where(hit, float(r), rank)
        x = jnp.where(hit, -jnp.inf, x)
        vals.append(m)
    return rank, vals


def _peer_select_kernel(qp_ref, sk_ref, r2_ref, e2_ref, n1_ref, e1_ref):
    for h in range(P_HEADS):
        def score(c):
            q = qp_ref[:, (2 * h + c) * N_KEYS:(2 * h + c + 1) * N_KEYS]
            return lax.dot_general(sk_ref[h, c], q, _NT, preferred_element_type=F32)

        s1 = score(0)
        s2 = score(1)
        rank1, top1 = _stable_topk(s1)
        rank2, top2 = _stable_topk(s2)
        cand = jnp.concatenate([top1[a] + top2[b] for a, b in _PAIRS], axis=0)
        crank, _ = _stable_topk(cand)
        chosen = crank < float(P_TOPK)
        best = top1[0] + top2[0]
        z = jnp.sum(jnp.where(chosen, jnp.exp(cand - best), 0.0), axis=0, keepdims=True)
        picked = jnp.where(chosen, 1.0, 0.0)
        n1 = jnp.zeros_like(s1)
        pos = 0
        for a in range(P_TOPK):
            width = sum(1 for pa, _ in _PAIRS if pa == a)
            count = jnp.sum(picked[pos:pos + width], axis=0, keepdims=True)
            n1 = jnp.where(rank1 == float(a), count, n1)
            pos += width
        r2_ref[h] = rank2
        e2_ref[h] = jnp.exp(s2 - top2[0])
        n1_ref[h] = n1
        e1_ref[h] = jnp.exp(s1 - top1[0]) / z


def _peer_select(qp, sub_keys_b, tm):
    t = qp.shape[0]
    out = jax.ShapeDtypeStruct((P_HEADS, N_KEYS, t), F32)
    blk = pl.BlockSpec((P_HEADS, N_KEYS, tm), lambda i: (0, 0, i))
    return pl.pallas_call(
        _peer_select_kernel,
        grid=(t // tm,),
        in_specs=[pl.BlockSpec((tm, D_MODEL), lambda i: (i, 0)),
                  _resident(sub_keys_b.shape)],
        out_specs=[blk] * 4,
        out_shape=[out] * 4,
        compiler_params=_params("arbitrary"),
        name="peer_select",
    )(qp, sub_keys_b)


def _gelu(h):
    return 0.5 * h * (1.0 + lax.erf(h * (2.0 ** -0.5)))


def _peer_dense_kernel(xn_ref, x1_ref, u_ref, v_ref, r2_ref, e2_ref, n1_ref, e1_ref,
                       o_ref, acc_ref, *, te):
    j = pl.program_id(1)

    @pl.when(j == 0)
    def _():
        acc_ref[...] = x1_ref[...]

    ht = lax.dot_general(u_ref[...], xn_ref[...], _NT, preferred_element_type=F32)
    blocks = []
    for kb in range(te // N_KEYS):
        i1 = j * (te // N_KEYS) + kb
        gate = jnp.zeros((N_KEYS, ht.shape[1]), F32)
        for h in range(P_HEADS):
            n1 = n1_ref[h, pl.ds(i1, 1), :]
            e1 = e1_ref[h, pl.ds(i1, 1), :]
            gate = gate + e1 * jnp.where(r2_ref[h] < n1, e2_ref[h], 0.0)
        blocks.append((_gelu(ht[kb * N_KEYS:(kb + 1) * N_KEYS]) * gate).astype(BF16))
    at = jnp.concatenate(blocks, axis=0)
    acc_ref[...] += lax.dot_general(at, v_ref[...], _TN, preferred_element_type=F32)

    @pl.when(j == pl.num_programs(1) - 1)
    def _():
        o_ref[...] = acc_ref[...]


def _peer_dense(xn, x1, eu_b, ev_b, r2, e2, n1, e1, tm, te):
    t = xn.shape[0]
    n_exp = eu_b.shape[0]
    tok = pl.BlockSpec((tm, D_MODEL), lambda i, j: (i, 0))
    exp = pl.BlockSpec((te, D_MODEL), lambda i, j: (j, 0))
    sel = pl.BlockSpec((P_HEADS, N_KEYS, tm), lambda i, j: (0, 0, i))
    return pl.pallas_call(
        functools.partial(_peer_dense_kernel, te=te),
        grid=(t // tm, n_exp // te),
        in_specs=[tok, tok, exp, exp, sel, sel, sel, sel],
        out_specs=tok,
        out_shape=jax.ShapeDtypeStruct((t, D_MODEL), F32),
        scratch_shapes=[pltpu.VMEM((tm, D_MODEL), F32)],
        compiler_params=_params("arbitrary", "arbitrary"),
        name="peer_dense",
    )(xn, x1, eu_b, ev_b, r2, e2, n1, e1)


def _tiles():
    return dict(proj_rows=256, attn_q=256, conv_rows=256, select_tokens=256,
                dense_tokens=512, dense_experts=512, sample_pages=4, sample_rows=128)


def _block_ones(width, group):
    r = jnp.arange(width) // group
    return (r[:, None] == r[None, :]).astype(BF16)


def _finish(x2d, a, c, w_out_b, norm_ffn_g, w_query_b, sub_keys_b, eu_b, ev_b, tiles, tm):
    x1, xn, qp = _out_proj(x2d, a, c, w_out_b, norm_ffn_g, w_query_b, tm)
    sel = _peer_select(qp, sub_keys_b, min(tiles["select_tokens"], x2d.shape[0]))
    return _peer_dense(xn, x1, eu_b, ev_b, *sel,
                       min(tiles["dense_tokens"], x2d.shape[0]), tiles["dense_experts"])


def kernel(x_prompt, x_sample, cache_k, cache_v, state_conv, page_table, norm_mix_g, w_in,
           q_norm_g, k_norm_g, lambda_q1, lambda_k1, lambda_q2, lambda_k2, subln_g, w_dw, b_dw,
           conv_norm_g, w_out, norm_ffn_g, w_query, sub_keys, expert_u, expert_v):
    tiles = _tiles()
    b, s, _ = x_prompt.shape
    bd = x_sample.shape[0]
    assert x_sample.shape[1] == 1

    w_in_b = w_in.astype(BF16)
    w_out_b = w_out.astype(BF16)
    w_query_b = w_query.astype(BF16)
    sub_keys_b = sub_keys.astype(BF16)
    eu_b = expert_u.astype(BF16)
    ev_b = expert_v.astype(BF16)
    row = lambda v: v.reshape(1, -1)
    qg = row(jnp.tile(q_norm_g, ATTN_W // HEAD_DIM))
    kg = row(jnp.tile(k_norm_g, ATTN_W // HEAD_DIM))
    lam4 = jnp.stack([lambda_q1, lambda_k1, lambda_q2, lambda_k2])
    pmat = _block_ones(2 * LANES, HEAD_DIM)
    lane = jnp.arange(HEAD_W)[:, None] // HEAD_DIM
    seg = (lane == (jnp.arange(2 * HEAD_W)[None, :] // HEAD_W)).astype(BF16)
    sg, bdw, cg = row(subln_g), row(b_dw), row(conv_norm_g)
    nmix, nffn = row(norm_mix_g), row(norm_ffn_g)

    xp = x_prompt.reshape(b * s, D_MODEL)
    qb, k, kb, v, vb, u = _in_proj(xp, nmix, w_in_b, qg, kg, pmat, tiles["proj_rows"])
    r3 = lambda y: y.reshape(b, s, -1)
    a_p = _attn_prompt(lam4, sg, r3(qb), r3(kb), r3(vb), tiles["attn_q"])
    c_p = _conv_prompt(r3(u), w_dw, bdw, cg, tiles["conv_rows"])
    y_p = _finish(xp, a_p.reshape(b * s, ATTN_W), c_p.reshape(b * s, CONV_W), w_out_b, nffn,
                  w_query_b, sub_keys_b, eu_b, ev_b, tiles, tiles["proj_rows"])

    rows = tiles["sample_rows"]
    xs = jnp.pad(x_sample.reshape(bd, D_MODEL), ((0, rows - bd), (0, 0)))
    qs, ks, _, vs, _, us = _in_proj(xs, nmix, w_in_b, qg, kg, pmat, rows)
    ks, vs, us = ks[:bd], vs[:bd], us[:bd]
    h3 = lambda y: y.reshape(bd, N_HEADS, HEAD_W)
    a_s = _attn_sample(page_table, lam4, sg, seg, h3(qs[:bd].astype(F32)), h3(ks), h3(vs),
                       cache_k, cache_v, tiles["sample_pages"])
    c_s = _conv_sample(state_conv, us, w_dw, bdw, cg)
    pad = lambda y: jnp.pad(y, ((0, rows - bd), (0, 0)))
    y_s = _finish(xs, pad(a_s.reshape(bd, ATTN_W)), pad(c_s), w_out_b, nffn,
                  w_query_b, sub_keys_b, eu_b, ev_b, tiles, rows)[:bd]

    heads = lambda y, n: y.reshape(n, -1, N_HEADS, HEAD_W)
    conv_prompt = r3(u)[:, s - CONV_STATE:]
    conv_sample = jnp.concatenate([state_conv[:, 1:], us[:, None, :]], axis=1)
    return (y_p.reshape(b, s, D_MODEL), y_s.reshape(bd, 1, D_MODEL),
            heads(k, b), heads(v, b), conv_prompt,
            heads(ks, bd), heads(vs, bd), conv_sample)
```

```python
import functools
import math

import jax
import jax.numpy as jnp
from jax import lax
from jax.experimental import pallas as pl
from jax.experimental.pallas import tpu as pltpu

F32 = jnp.float32
BF16 = jnp.bfloat16

D_MODEL = 2048
N_HEADS = 8
HEAD_DIM = 64
HEAD_W = 2 * HEAD_DIM
ATTN_W = N_HEADS * HEAD_W
CONV_W = D_MODEL - ATTN_W
CONV_WIDTH = 31
CONV_STATE = CONV_WIDTH - 1
LAMBDA_INIT = 0.8 - 0.6 * math.exp(-0.3 * (1 - 1))
SCALE = HEAD_DIM ** -0.5
PAGE_SIZE = 128
N_KEYS = 128
P_HEADS = 8
P_TOPK = 16
EPS = 1e-6
NEG = -1e30
LOG2E = math.log2(math.e)

SUBLANES = 8
LANES = 128
VMEM_LIMIT_BYTES = 56 * 1024 * 1024

_NT = (((1,), (1,)), ((), ()))
_TN = (((0,), (0,)), ((), ()))


def _params(*sem):
    return pltpu.CompilerParams(dimension_semantics=sem, vmem_limit_bytes=VMEM_LIMIT_BYTES)


def _resident(shape):
    return pl.BlockSpec(shape, lambda *_: (0,) * len(shape), pipeline_mode=pl.Buffered(1))


def _rms(x, gain):
    ms = jnp.mean(x * x, axis=-1, keepdims=True)
    return x * lax.rsqrt(ms + EPS) * gain


def _sigmoid(x):
    return 1.0 / (1.0 + jnp.exp(-x))


def _diff_lambda(lam_ref):
    l = lam_ref[...]
    e1 = jnp.exp(jnp.sum(l[0:1] * l[1:2], axis=-1, keepdims=True))
    e2 = jnp.exp(jnp.sum(l[2:3] * l[3:4], axis=-1, keepdims=True))
    return e1 - e2 + LAMBDA_INIT


def _in_proj_kernel(x_ref, g_ref, w_ref, qg_ref, kg_ref, p_ref,
                    qb_ref, k_ref, kb_ref, v_ref, vb_ref, u_ref):
    xn = _rms(x_ref[...], g_ref[...]).astype(BF16)

    def proj(col):
        return jnp.dot(xn, w_ref[:, col:col + ATTN_W], preferred_element_type=F32)

    def head_norm(y, gain):
        y2 = y * y
        hi = y2.astype(BF16)
        lo = (y2 - hi.astype(F32)).astype(BF16)
        p = p_ref[...]
        w = p.shape[0]
        parts = []
        for c in range(ATTN_W // w):
            sl = slice(c * w, (c + 1) * w)
            parts.append(jnp.dot(hi[:, sl], p, preferred_element_type=F32)
                         + jnp.dot(lo[:, sl], p, preferred_element_type=F32))
        ss = jnp.concatenate(parts, axis=-1)
        return y * lax.rsqrt(ss * (1.0 / HEAD_DIM) + EPS) * gain

    q = head_norm(proj(0), qg_ref[...])
    qb_ref[...] = (q * (SCALE * LOG2E)).astype(BF16)
    k = head_norm(proj(ATTN_W), kg_ref[...])
    k_ref[...] = k
    kb_ref[...] = k.astype(BF16)
    v = proj(2 * ATTN_W)
    v_ref[...] = v
    vb_ref[...] = v.astype(BF16)
    a = proj(3 * ATTN_W)
    gt = proj(3 * ATTN_W + CONV_W)
    u_ref[...] = a * _sigmoid(gt)


def _in_proj(x2d, norm_g, w_in_b, qg, kg, pmat, tm):
    t = x2d.shape[0]
    row = lambda i: (i, 0)
    blk = lambda: pl.BlockSpec((tm, ATTN_W), row)
    return pl.pallas_call(
        _in_proj_kernel,
        grid=(t // tm,),
        in_specs=[pl.BlockSpec((tm, D_MODEL), row),
                  _resident((1, D_MODEL)),
                  _resident(w_in_b.shape),
                  _resident((1, ATTN_W)),
                  _resident((1, ATTN_W)),
                  _resident(pmat.shape)],
        out_specs=[blk(), blk(), blk(), blk(), blk(), blk()],
        out_shape=[jax.ShapeDtypeStruct((t, ATTN_W), BF16),
                   jax.ShapeDtypeStruct((t, ATTN_W), F32),
                   jax.ShapeDtypeStruct((t, ATTN_W), BF16),
                   jax.ShapeDtypeStruct((t, ATTN_W), F32),
                   jax.ShapeDtypeStruct((t, ATTN_W), BF16),
                   jax.ShapeDtypeStruct((t, CONV_W), F32)],
        compiler_params=_params("arbitrary"),
        name="in_proj",
    )(x2d, norm_g, w_in_b, qg, kg, pmat)


def _lane_fold(x, op):
    out = x[:, :LANES]
    for c in range(1, x.shape[1] // LANES):
        out = op(out, x[:, c * LANES:(c + 1) * LANES])
    return out


def _attn_prompt_kernel(lam_ref, sg_ref, q_ref, k_ref, v_ref, o_ref, s1_ref, s2_ref, *, tq):
    lam = _diff_lambda(lam_ref)
    seq = q_ref.shape[1]
    lane = lax.broadcasted_iota(jnp.int32, (tq, HEAD_W), 1)
    r = lax.broadcasted_iota(jnp.int32, (tq, tq), 0)
    c = lax.broadcasted_iota(jnp.int32, (tq, tq), 1)
    keep = c <= r
    for i in range(seq // tq):
        rows = slice(i * tq, (i + 1) * tq)
        q = q_ref[0, rows, :]
        zero = jnp.zeros_like(q)
        qs = (jnp.where(lane < HEAD_DIM, q, zero), jnp.where(lane >= HEAD_DIM, q, zero))
        tops = [jnp.full((tq, LANES), NEG, F32) for _ in qs]
        for j in range(i + 1):
            cols = slice(j * tq, (j + 1) * tq)
            kj = k_ref[0, cols, :]
            for n, s_ref in enumerate((s1_ref, s2_ref)):
                s = lax.dot_general(qs[n], kj, _NT, preferred_element_type=F32)
                if j == i:
                    s = jnp.where(keep, s, NEG)
                s_ref[:, cols] = s
                tops[n] = jnp.maximum(tops[n], _lane_fold(s, jnp.maximum))
        outs = []
        for n, s_ref in enumerate((s1_ref, s2_ref)):
            m = jnp.max(tops[n], axis=-1, keepdims=True)
            part = jnp.zeros((tq, LANES), F32)
            acc = jnp.zeros((tq, HEAD_W), F32)
            for j in range(i + 1):
                cols = slice(j * tq, (j + 1) * tq)
                p = jnp.exp2(s_ref[:, cols] - m)
                part = part + _lane_fold(p, jnp.add)
                acc = acc + jnp.dot(p.astype(BF16), v_ref[0, cols, :],
                                    preferred_element_type=F32)
            outs.append(acc / jnp.sum(part, axis=-1, keepdims=True))
        o = outs[0] - lam * outs[1]
        o_ref[0, rows, :] = (_rms(o, sg_ref[...]) * (1.0 - LAMBDA_INIT)).astype(o_ref.dtype)


def _attn_prompt(lam4, subln_g, qb, kb, vb, tq):
    b, s, _ = qb.shape
    head = pl.BlockSpec((1, s, HEAD_W), lambda bi, h: (bi, 0, h))
    return pl.pallas_call(
        functools.partial(_attn_prompt_kernel, tq=tq),
        grid=(b, N_HEADS),
        in_specs=[_resident(lam4.shape), _resident((1, HEAD_W)), head, head, head],
        out_specs=head,
        out_shape=jax.ShapeDtypeStruct((b, s, ATTN_W), BF16),
        scratch_shapes=[pltpu.VMEM((tq, s), F32), pltpu.VMEM((tq, s), F32)],
        compiler_params=_params("arbitrary", "arbitrary"),
        name="attn_prompt",
    )(lam4, subln_g, qb, kb, vb)


SAMPLE_TOKENS = 8


def _attn_sample_kernel(pt_ref, lam_ref, sg_ref, seg_ref, q_ref, kn_ref, vn_ref, *rest, pages):
    k_refs = rest[:pages]
    v_refs = rest[pages:2 * pages]
    o_ref = rest[2 * pages]
    m_ref, l_ref, acc_ref = rest[2 * pages + 1:]
    p = pl.program_id(1)
    q = q_ref[0]
    seg = seg_ref[...]

    def scores(kblk):
        n = kblk.shape[0]
        prod = (kblk * q[None]).reshape(n * N_HEADS, HEAD_W).astype(BF16)
        s = jnp.dot(prod, seg, preferred_element_type=F32)
        return s.reshape(n, N_HEADS, 2 * HEAD_W)

    @pl.when(p == 0)
    def _():
        m_ref[...] = scores(kn_ref[...])[0]
        l_ref[...] = jnp.ones_like(l_ref)
        vn = vn_ref[0]
        acc_ref[...] = jnp.concatenate([vn, vn], axis=-1)

    m, l, acc = m_ref[...], l_ref[...], acc_ref[...]
    for r in range(pages):
        for c in range(PAGE_SIZE // SAMPLE_TOKENS):
            toks = slice(c * SAMPLE_TOKENS, (c + 1) * SAMPLE_TOKENS)
            s = scores(k_refs[r][0, toks])
            v = v_refs[r][0, toks]
            m_new = jnp.maximum(m, jnp.max(s, axis=0))
            corr = jnp.exp2(m - m_new)
            pr = jnp.exp2(s - m_new[None])
            l = l * corr + jnp.sum(pr, axis=0)
            pv = jnp.concatenate([jnp.sum(pr[..., :HEAD_W] * v, axis=0),
                                  jnp.sum(pr[..., HEAD_W:] * v, axis=0)], axis=-1)
            acc = acc * corr + pv
            m = m_new
    m_ref[...], l_ref[...], acc_ref[...] = m, l, acc

    @pl.when(p == pl.num_programs(1) - 1)
    def _():
        o = acc_ref[...] / l_ref[...]
        w = o[:, :HEAD_W] - _diff_lambda(lam_ref) * o[:, HEAD_W:]
        o_ref[0] = (_rms(w, sg_ref[...]) * (1.0 - LAMBDA_INIT)).astype(o_ref.dtype)


def _attn_sample(page_table, lam4, subln_g, seg, q, k_new, v_new, cache_k, cache_v, pages):
    bd, n_pages = page_table.shape
    page_blk = (1, PAGE_SIZE, N_HEADS, HEAD_W)

    def page_spec(r):
        return pl.BlockSpec(page_blk, lambda b, p, pt: (pt[b, p * pages + r], 0, 0, 0))

    tok = pl.BlockSpec((1, N_HEADS, HEAD_W), lambda b, p, pt: (b, 0, 0))
    const = lambda shape: pl.BlockSpec(shape, lambda b, p, pt: (0,) * len(shape))
    grid_spec = pltpu.PrefetchScalarGridSpec(
        num_scalar_prefetch=1,
        grid=(bd, n_pages // pages),
        in_specs=[const(lam4.shape), const((1, HEAD_W)), const(seg.shape), tok, tok, tok]
                 + [page_spec(r) for r in range(pages)]
                 + [page_spec(r) for r in range(pages)],
        out_specs=tok,
        scratch_shapes=[pltpu.VMEM((N_HEADS, 2 * HEAD_W), F32),
                        pltpu.VMEM((N_HEADS, 2 * HEAD_W), F32),
                        pltpu.VMEM((N_HEADS, 2 * HEAD_W), F32)],
    )
    return pl.pallas_call(
        functools.partial(_attn_sample_kernel, pages=pages),
        grid_spec=grid_spec,
        out_shape=jax.ShapeDtypeStruct((bd, N_HEADS, HEAD_W), BF16),
        compiler_params=_params("arbitrary", "arbitrary"),
        name="attn_sample",
    )(page_table, lam4, subln_g, seg, q, k_new, v_new,
      *([cache_k] * pages), *([cache_v] * pages))


CONV_HALO = 32
CONV_ROWS = 64


def _conv_finish(y, g_ref):
    c = _rms(y, g_ref[...])
    return (c * _sigmoid(c)).astype(BF16)


def _conv_prompt_kernel(u_ref, halo_ref, w_ref, b_ref, g_ref, o_ref, win_ref, y_ref, *, ts):
    i = pl.program_id(1)
    halo = halo_ref[0]
    win_ref[0:CONV_HALO, :] = jnp.where(i == 0, jnp.zeros_like(halo), halo)
    win_ref[CONV_HALO:, :] = u_ref[0]
    shift = CONV_HALO - CONV_STATE

    def lane_chunk(c, _):
        lanes = pl.ds(pl.multiple_of(c * LANES, LANES), LANES)
        bias = b_ref[:, lanes]
        for r in range(ts // CONV_ROWS):
            acc = jnp.zeros((CONV_ROWS, LANES), F32) + bias
            for j in range(CONV_WIDTH):
                row0 = r * CONV_ROWS + j + shift
                acc = acc + w_ref[j:j + 1, lanes] * win_ref[row0:row0 + CONV_ROWS, lanes]
            y_ref[r * CONV_ROWS:(r + 1) * CONV_ROWS, lanes] = acc
        return 0

    lax.fori_loop(0, CONV_W // LANES, lane_chunk, 0)
    o_ref[0] = _conv_finish(y_ref[...], g_ref)


def _conv_prompt(u, w_dw, b_dw, g, ts):
    b, s, _ = u.shape
    per = ts // CONV_HALO
    return pl.pallas_call(
        functools.partial(_conv_prompt_kernel, ts=ts),
        grid=(b, s // ts),
        in_specs=[pl.BlockSpec((1, ts, CONV_W), lambda bi, i: (bi, i, 0)),
                  pl.BlockSpec((1, CONV_HALO, CONV_W),
                               lambda bi, i: (bi, jnp.maximum(i * per - 1, 0), 0)),
                  _resident(w_dw.shape),
                  _resident((1, CONV_W)),
                  _resident((1, CONV_W))],
        out_specs=pl.BlockSpec((1, ts, CONV_W), lambda bi, i: (bi, i, 0)),
        out_shape=jax.ShapeDtypeStruct((b, s, CONV_W), BF16),
        scratch_shapes=[pltpu.VMEM((CONV_HALO + ts, CONV_W), F32),
                        pltpu.VMEM((ts, CONV_W), F32)],
        compiler_params=_params("arbitrary", "arbitrary"),
        name="conv_prompt",
    )(u, u, w_dw, b_dw, g)


def _conv_sample_kernel(st_ref, u_ref, w_ref, b_ref, g_ref, o_ref):
    acc = b_ref[...] + w_ref[CONV_STATE:CONV_WIDTH, :] * u_ref[...]
    for j in range(CONV_STATE):
        acc = acc + w_ref[j:j + 1, :] * st_ref[:, j, :]
    o_ref[...] = _conv_finish(acc, g_ref)


def _conv_sample(state, u, w_dw, b_dw, g):
    bd = u.shape[0]
    return pl.pallas_call(
        _conv_sample_kernel,
        out_shape=jax.ShapeDtypeStruct((bd, CONV_W), BF16),
        compiler_params=pltpu.CompilerParams(vmem_limit_bytes=VMEM_LIMIT_BYTES),
        name="conv_sample",
    )(state, u, w_dw, b_dw, g)


def _out_proj_kernel(x_ref, a_ref, c_ref, wo_ref, g_ref, wq_ref, x1_ref, xnt_ref, qp_ref):
    x1 = (x_ref[...]
          + jnp.dot(a_ref[...], wo_ref[0:ATTN_W, :], preferred_element_type=F32)
          + jnp.dot(c_ref[...], wo_ref[ATTN_W:D_MODEL, :], preferred_element_type=F32))
    x1_ref[...] = x1
    xn = _rms(x1, g_ref[...])
    xnt_ref[...] = xn.T.astype(BF16)
    qp_ref[...] = jnp.dot(xn.astype(BF16), wq_ref[...],
                          preferred_element_type=F32).astype(BF16)


def _out_proj(x2d, a, c, w_out_b, g, w_query_b, tm):
    t = x2d.shape[0]
    row = lambda i: (i, 0)
    return pl.pallas_call(
        _out_proj_kernel,
        grid=(t // tm,),
        in_specs=[pl.BlockSpec((tm, D_MODEL), row),
                  pl.BlockSpec((tm, ATTN_W), row),
                  pl.BlockSpec((tm, CONV_W), row),
                  _resident(w_out_b.shape),
                  _resident((1, D_MODEL)),
                  _resident(w_query_b.shape)],
        out_specs=[pl.BlockSpec((tm, D_MODEL), row),
                   pl.BlockSpec((D_MODEL, tm), lambda i: (0, i)),
                   pl.BlockSpec((tm, D_MODEL), row)],
        out_shape=[jax.ShapeDtypeStruct((t, D_MODEL), F32),
                   jax.ShapeDtypeStruct((D_MODEL, t), BF16),
                   jax.ShapeDtypeStruct((t, D_MODEL), BF16)],
        compiler_params=_params("arbitrary"),
        name="out_proj",
    )(x2d, a, c, w_out_b, g, w_query_b)


_PAIR_WIDTHS = [P_TOPK // (a + 1) for a in range(P_TOPK)]
N_PAIRS = sum(_PAIR_WIDTHS)
PAIR_ROWS = -(-N_PAIRS // SUBLANES) * SUBLANES


def _topk_ranks(x, stable):
    rows = x.shape[0]
    rank = jnp.full(x.shape, float(P_TOPK), F32)
    if stable:
        row = lax.broadcasted_iota(jnp.int32, x.shape, 0).astype(F32)
    vals = []
    for r in range(P_TOPK):
        m = jnp.max(x, axis=0, keepdims=True)
        if stable:
            first = jnp.min(jnp.where(x == m, row, float(rows)), axis=0, keepdims=True)
            hit = row == first
        else:
            hit = x == m
        rank = jnp.where(hit, float(r), rank)
        x = jnp.where(hit, -jnp.inf, x)
        vals.append(m)
    ranked = jnp.sum(jnp.where(rank < float(P_TOPK), 1.0, 0.0), axis=0, keepdims=True)
    return rank, vals, ranked


def _peer_select_kernel(qp_ref, sk_ref, r2_ref, e2_ref, n1_ref, e1_ref,
                        t2_ref, cand_ref, pick_ref):
    tm = qp_ref.shape[0]
    cand_ref[N_PAIRS:, :] = jnp.full((PAIR_ROWS - N_PAIRS, tm), -jnp.inf, F32)

    def head(h, stable):
        def score(c):
            q = qp_ref[:, (2 * h + c) * N_KEYS:(2 * h + c + 1) * N_KEYS]
            return lax.dot_general(sk_ref[h, c], q, _NT, preferred_element_type=F32)

        s1 = score(0)
        s2 = score(1)
        rank1, top1, ranked1 = _topk_ranks(s1, stable)
        rank2, top2, ranked2 = _topk_ranks(s2, stable)
        for r in range(P_TOPK):
            t2_ref[r:r + 1, :] = top2[r]
        pos = 0
        for a, width in enumerate(_PAIR_WIDTHS):
            cand_ref[pos:pos + width, :] = top1[a] + t2_ref[0:width, :]
            pos += width
        cand = cand_ref[...]
        crank, _, ranked3 = _topk_ranks(cand, stable)
        chosen = crank < float(P_TOPK)
        best = top1[0] + top2[0]
        z = jnp.sum(jnp.where(chosen, jnp.exp(cand - best), 0.0), axis=0, keepdims=True)
        pick_ref[...] = jnp.where(chosen, 1.0, 0.0)
        n1 = jnp.zeros_like(s1)
        pos = 0
        for a, width in enumerate(_PAIR_WIDTHS):
            count = jnp.sum(pick_ref[pos:pos + width, :], axis=0, keepdims=True)
            n1 = jnp.where(rank1 == float(a), count, n1)
            pos += width
        r2_ref[h] = rank2.astype(BF16)
        e2_ref[h] = jnp.exp(s2 - top2[0]).astype(BF16)
        n1_ref[h] = n1
        e1_ref[h] = jnp.exp(s1 - top1[0]) / z
        k = float(P_TOPK)
        return jnp.abs(ranked1 - k) + jnp.abs(ranked2 - k) + jnp.abs(ranked3 - k)

    for h in range(P_HEADS):
        ties = head(h, False)

        @pl.when(jnp.max(ties) > 0.0)
        def _():
            head(h, True)


def _peer_select(qp, sub_keys_b, tm):
    t = qp.shape[0]
    blk = pl.BlockSpec((P_HEADS, N_KEYS, tm), lambda i: (0, 0, i))
    out = lambda dt: jax.ShapeDtypeStruct((P_HEADS, N_KEYS, t), dt)
    return pl.pallas_call(
        _peer_select_kernel,
        grid=(t // tm,),
        in_specs=[pl.BlockSpec((tm, D_MODEL), lambda i: (i, 0)),
                  _resident(sub_keys_b.shape)],
        out_specs=[blk] * 4,
        out_shape=[out(BF16), out(BF16), out(F32), out(F32)],
        scratch_shapes=[pltpu.VMEM((P_TOPK, tm), F32),
                        pltpu.VMEM((PAIR_ROWS, tm), F32),
                        pltpu.VMEM((PAIR_ROWS, tm), F32)],
        compiler_params=_params("arbitrary"),
        name="peer_select",
    )(qp, sub_keys_b)


def _gelu(h):
    return 0.5 * h * (1.0 + lax.erf(h * (2.0 ** -0.5)))


BF16_ROWS = 2 * SUBLANES


def _peer_dense_kernel(xnt_ref, x1_ref, u_ref, v_ref, r2_ref, e2_ref, n1_ref, e1_ref,
                       o_ref, *, te):
    j = pl.program_id(1)
    tm = xnt_ref.shape[1]

    @pl.when(j == 0)
    def _():
        o_ref[...] = x1_ref[...]

    ht = jnp.dot(u_ref[...], xnt_ref[...], preferred_element_type=F32)
    zero = jnp.zeros((), BF16)
    blocks = []
    for kb in range(te // N_KEYS):
        i1 = j * (te // N_KEYS) + kb

        def token_row(ref, h):
            row = jnp.broadcast_to(ref[h, pl.ds(i1, 1), :], (BF16_ROWS, tm))
            return row.astype(BF16)[None]

        gate = None
        for h in range(P_HEADS):
            term = token_row(e1_ref, h) * jnp.where(r2_ref[h] < token_row(n1_ref, h),
                                                    e2_ref[h], zero)
            gate = term if gate is None else gate + term
        act = _gelu(ht[kb * N_KEYS:(kb + 1) * N_KEYS]).astype(BF16).reshape(gate.shape)
        blocks.append((act * gate).reshape(N_KEYS, tm))
    at = jnp.concatenate(blocks, axis=0)
    o_ref[...] += lax.dot_general(at, v_ref[...], _TN, preferred_element_type=F32)


def _peer_dense(xnt, x1, eu_b, ev_b, r2, e2, n1, e1, tm, te):
    t = x1.shape[0]
    n_exp = eu_b.shape[0]
    once = pl.Buffered(1)
    tok = lambda i, j: (i, 0)
    exp = pl.BlockSpec((te, D_MODEL), lambda i, j: (j, 0))
    packed = pl.BlockSpec((P_HEADS, N_KEYS // BF16_ROWS, BF16_ROWS, tm),
                          lambda i, j: (0, 0, 0, i), pipeline_mode=once)
    rows = pl.BlockSpec((P_HEADS, N_KEYS, tm), lambda i, j: (0, 0, i), pipeline_mode=once)
    split = lambda y: y.reshape(P_HEADS, N_KEYS // BF16_ROWS, BF16_ROWS, t)
    return pl.pallas_call(
        functools.partial(_peer_dense_kernel, te=te),
        grid=(t // tm, n_exp // te),
        in_specs=[pl.BlockSpec((D_MODEL, tm), lambda i, j: (0, i), pipeline_mode=once),
                  pl.BlockSpec((tm, D_MODEL), tok, pipeline_mode=once),
                  exp, exp, packed, packed, rows, rows],
        out_specs=pl.BlockSpec((tm, D_MODEL), tok),
        out_shape=jax.ShapeDtypeStruct((t, D_MODEL), F32),
        compiler_params=_params("arbitrary", "arbitrary"),
        name="peer_dense",
    )(xnt, x1, eu_b, ev_b, split(r2), split(e2), n1, e1)


def _tiles():
    return dict(proj_rows=256, attn_q=512, conv_rows=256, select_tokens=256,
                dense_tokens=512, dense_experts=1024, sample_pages=8, sample_rows=128)


def _block_ones(width, group):
    r = jnp.arange(width) // group
    return (r[:, None] == r[None, :]).astype(BF16)


def _finish(x2d, a, c, w_out_b, norm_ffn_g, w_query_b, sub_keys_b, eu_b, ev_b, tiles, tm):
    x1, xnt, qp = _out_proj(x2d, a, c, w_out_b, norm_ffn_g, w_query_b, tm)
    sel = _peer_select(qp, sub_keys_b, min(tiles["select_tokens"], x2d.shape[0]))
    return _peer_dense(xnt, x1, eu_b, ev_b, *sel,
                       min(tiles["dense_tokens"], x2d.shape[0]), tiles["dense_experts"])


def kernel(x_prompt, x_sample, cache_k, cache_v, state_conv, page_table, norm_mix_g, w_in,
           q_norm_g, k_norm_g, lambda_q1, lambda_k1, lambda_q2, lambda_k2, subln_g, w_dw, b_dw,
           conv_norm_g, w_out, norm_ffn_g, w_query, sub_keys, expert_u, expert_v):
    tiles = _tiles()
    b, s, _ = x_prompt.shape
    bd = x_sample.shape[0]
    assert x_sample.shape[1] == 1

    w_in_b = w_in.astype(BF16)
    w_out_b = w_out.astype(BF16)
    w_query_b = w_query.astype(BF16)
    sub_keys_b = sub_keys.astype(BF16)
    eu_b = expert_u.astype(BF16)
    ev_b = expert_v.astype(BF16)
    row = lambda v: v.reshape(1, -1)
    qg = row(jnp.tile(q_norm_g, ATTN_W // HEAD_DIM))
    kg = row(jnp.tile(k_norm_g, ATTN_W // HEAD_DIM))
    lam4 = jnp.stack([lambda_q1, lambda_k1, lambda_q2, lambda_k2])
    pmat = _block_ones(2 * LANES, HEAD_DIM)
    lane = jnp.arange(HEAD_W)[:, None] // HEAD_DIM
    seg = (lane == (jnp.arange(2 * HEAD_W)[None, :] // HEAD_W)).astype(BF16)
    sg, bdw, cg = row(subln_g), row(b_dw), row(conv_norm_g)
    nmix, nffn = row(norm_mix_g), row(norm_ffn_g)

    xp = x_prompt.reshape(b * s, D_MODEL)
    qb, k, kb, v, vb, u = _in_proj(xp, nmix, w_in_b, qg, kg, pmat, tiles["proj_rows"])
    r3 = lambda y: y.reshape(b, s, -1)
    a_p = _attn_prompt(lam4, sg, r3(qb), r3(kb), r3(vb), tiles["attn_q"])
    c_p = _conv_prompt(r3(u), w_dw, bdw, cg, tiles["conv_rows"])
    y_p = _finish(xp, a_p.reshape(b * s, ATTN_W), c_p.reshape(b * s, CONV_W), w_out_b, nffn,
                  w_query_b, sub_keys_b, eu_b, ev_b, tiles, tiles["proj_rows"])

    rows = tiles["sample_rows"]
    xs = jnp.pad(x_sample.reshape(bd, D_MODEL), ((0, rows - bd), (0, 0)))
    qs, ks, _, vs, _, us = _in_proj(xs, nmix, w_in_b, qg, kg, pmat, rows)
    ks, vs, us = ks[:bd], vs[:bd], us[:bd]
    h3 = lambda y: y.reshape(bd, N_HEADS, HEAD_W)
    a_s = _attn_sample(page_table, lam4, sg, seg, h3(qs[:bd].astype(F32)), h3(ks), h3(vs),
                       cache_k, cache_v, tiles["sample_pages"])
    c_s = _conv_sample(state_conv, us, w_dw, bdw, cg)
    pad = lambda y: jnp.pad(y, ((0, rows - bd), (0, 0)))
    y_s = _finish(xs, pad(a_s.reshape(bd, ATTN_W)), pad(c_s), w_out_b, nffn,
                  w_query_b, sub_keys_b, eu_b, ev_b, tiles, rows)[:bd]

    heads = lambda y, n: y.reshape(n, -1, N_HEADS, HEAD_W)
    conv_prompt = r3(u)[:, s - CONV_STATE:]
    conv_sample = jnp.concatenate([state_conv[:, 1:], us[:, None, :]], axis=1)
    return (y_p.reshape(b, s, D_MODEL), y_s.reshape(bd, 1, D_MODEL),
            heads(k, b), heads(v, b), conv_prompt,
            heads(ks, bd), heads(vs, bd), conv_sample)
```

```python
import functools
import math

import jax
import jax.numpy as jnp
from jax import lax
from jax.experimental import pallas as pl
from jax.experimental.pallas import tpu as pltpu

F32 = jnp.float32
BF16 = jnp.bfloat16

D_MODEL = 2048
N_HEADS = 8
HEAD_DIM = 64
HEAD_W = 2 * HEAD_DIM
ATTN_W = N_HEADS * HEAD_W
CONV_W = D_MODEL - ATTN_W
CONV_WIDTH = 31
CONV_STATE = CONV_WIDTH - 1
LAMBDA_INIT = 0.8 - 0.6 * math.exp(-0.3 * (1 - 1))
SCALE = HEAD_DIM ** -0.5
PAGE_SIZE = 128
N_KEYS = 128
P_HEADS = 8
P_TOPK = 16
EPS = 1e-6
NEG = -1e30
LOG2E = math.log2(math.e)

SUBLANES = 8
LANES = 128
VMEM_LIMIT_BYTES = 56 * 1024 * 1024

_NT = (((1,), (1,)), ((), ()))
_TN = (((0,), (0,)), ((), ()))


def _params(*sem):
    return pltpu.CompilerParams(dimension_semantics=sem, vmem_limit_bytes=VMEM_LIMIT_BYTES)


def _resident(shape):
    return pl.BlockSpec(shape, lambda *_: (0,) * len(shape), pipeline_mode=pl.Buffered(1))


def _rms(x, gain):
    ms = jnp.mean(x * x, axis=-1, keepdims=True)
    return x * lax.rsqrt(ms + EPS) * gain


def _sigmoid(x):
    return 1.0 / (1.0 + jnp.exp(-x))


def _diff_lambda(lam_ref):
    l = lam_ref[...]
    e1 = jnp.exp(jnp.sum(l[0:1] * l[1:2], axis=-1, keepdims=True))
    e2 = jnp.exp(jnp.sum(l[2:3] * l[3:4], axis=-1, keepdims=True))
    return e1 - e2 + LAMBDA_INIT


def _in_proj_kernel(x_ref, g_ref, w_ref, qg_ref, kg_ref, p_ref,
                    qb_ref, k_ref, kb_ref, v_ref, vb_ref, u_ref):
    xn = _rms(x_ref[...], g_ref[...]).astype(BF16)

    def proj(col):
        return jnp.dot(xn, w_ref[:, col:col + ATTN_W], preferred_element_type=F32)

    def head_norm(y, gain):
        y2 = y * y
        hi = y2.astype(BF16)
        lo = (y2 - hi.astype(F32)).astype(BF16)
        p = p_ref[...]
        w = p.shape[0]
        parts = []
        for c in range(ATTN_W // w):
            sl = slice(c * w, (c + 1) * w)
            parts.append(jnp.dot(hi[:, sl], p, preferred_element_type=F32)
                         + jnp.dot(lo[:, sl], p, preferred_element_type=F32))
        ss = jnp.concatenate(parts, axis=-1)
        return y * lax.rsqrt(ss * (1.0 / HEAD_DIM) + EPS) * gain

    q = head_norm(proj(0), qg_ref[...])
    qb_ref[...] = (q * (SCALE * LOG2E)).astype(BF16)
    k = head_norm(proj(ATTN_W), kg_ref[...])
    k_ref[...] = k
    kb_ref[...] = k.astype(BF16)
    v = proj(2 * ATTN_W)
    v_ref[...] = v
    vb_ref[...] = v.astype(BF16)
    a = proj(3 * ATTN_W)
    gt = proj(3 * ATTN_W + CONV_W)
    u_ref[...] = a * _sigmoid(gt)


def _in_proj(x2d, norm_g, w_in_b, qg, kg, pmat, tm):
    t = x2d.shape[0]
    row = lambda i: (i, 0)
    blk = lambda: pl.BlockSpec((tm, ATTN_W), row)
    return pl.pallas_call(
        _in_proj_kernel,
        grid=(t // tm,),
        in_specs=[pl.BlockSpec((tm, D_MODEL), row),
                  _resident((1, D_MODEL)),
                  _resident(w_in_b.shape),
                  _resident((1, ATTN_W)),
                  _resident((1, ATTN_W)),
                  _resident(pmat.shape)],
        out_specs=[blk(), blk(), blk(), blk(), blk(), blk()],
        out_shape=[jax.ShapeDtypeStruct((t, ATTN_W), BF16),
                   jax.ShapeDtypeStruct((t, ATTN_W), F32),
                   jax.ShapeDtypeStruct((t, ATTN_W), BF16),
                   jax.ShapeDtypeStruct((t, ATTN_W), F32),
                   jax.ShapeDtypeStruct((t, ATTN_W), BF16),
                   jax.ShapeDtypeStruct((t, CONV_W), F32)],
        compiler_params=_params("arbitrary"),
        name="in_proj",
    )(x2d, norm_g, w_in_b, qg, kg, pmat)


def _lane_fold(x, op):
    out = x[:, :LANES]
    for c in range(1, x.shape[1] // LANES):
        out = op(out, x[:, c * LANES:(c + 1) * LANES])
    return out


def _attn_prompt_kernel(lam_ref, sg_ref, q_ref, k_ref, v_ref, o_ref, s1_ref, s2_ref, *, tq):
    lam = _diff_lambda(lam_ref)
    seq = q_ref.shape[1]
    lane = lax.broadcasted_iota(jnp.int32, (tq, HEAD_W), 1)
    r = lax.broadcasted_iota(jnp.int32, (tq, tq), 0)
    c = lax.broadcasted_iota(jnp.int32, (tq, tq), 1)
    keep = c <= r
    for i in range(seq // tq):
        rows = slice(i * tq, (i + 1) * tq)
        q = q_ref[0, rows, :]
        zero = jnp.zeros_like(q)
        qs = (jnp.where(lane < HEAD_DIM, q, zero), jnp.where(lane >= HEAD_DIM, q, zero))
        tops = [jnp.full((tq, LANES), NEG, F32) for _ in qs]
        for j in range(i + 1):
            cols = slice(j * tq, (j + 1) * tq)
            kj = k_ref[0, cols, :]
            for n, s_ref in enumerate((s1_ref, s2_ref)):
                s = lax.dot_general(qs[n], kj, _NT, preferred_element_type=F32)
                if j == i:
                    s = jnp.where(keep, s, NEG)
                s_ref[:, cols] = s
                tops[n] = jnp.maximum(tops[n], _lane_fold(s, jnp.maximum))
        outs = []
        for n, s_ref in enumerate((s1_ref, s2_ref)):
            m = jnp.max(tops[n], axis=-1, keepdims=True)
            part = jnp.zeros((tq, LANES), F32)
            acc = jnp.zeros((tq, HEAD_W), F32)
            for j in range(i + 1):
                cols = slice(j * tq, (j + 1) * tq)
                p = jnp.exp2(s_ref[:, cols] - m)
                part = part + _lane_fold(p, jnp.add)
                acc = acc + jnp.dot(p.astype(BF16), v_ref[0, cols, :],
                                    preferred_element_type=F32)
            outs.append(acc / jnp.sum(part, axis=-1, keepdims=True))
        o = outs[0] - lam * outs[1]
        o_ref[0, rows, :] = (_rms(o, sg_ref[...]) * (1.0 - LAMBDA_INIT)).astype(o_ref.dtype)


def _attn_prompt(lam4, subln_g, qb, kb, vb, tq):
    b, s, _ = qb.shape
    head = pl.BlockSpec((1, s, HEAD_W), lambda bi, h: (bi, 0, h))
    return pl.pallas_call(
        functools.partial(_attn_prompt_kernel, tq=tq),
        grid=(b, N_HEADS),
        in_specs=[_resident(lam4.shape), _resident((1, HEAD_W)), head, head, head],
        out_specs=head,
        out_shape=jax.ShapeDtypeStruct((b, s, ATTN_W), BF16),
        scratch_shapes=[pltpu.VMEM((tq, s), F32), pltpu.VMEM((tq, s), F32)],
        compiler_params=_params("arbitrary", "arbitrary"),
        name="attn_prompt",
    )(lam4, subln_g, qb, kb, vb)


def _attn_sample_kernel(pt_ref, lam_ref, sg_ref, q_ref, kn_ref, vn_ref, *rest, pages):
    k_refs = rest[:pages]
    v_refs = rest[pages:2 * pages]
    o_ref = rest[2 * pages]
    m_ref, l_ref, acc_ref = rest[2 * pages + 1:]
    p = pl.program_id(1)
    q = q_ref[0]
    lane = lax.broadcasted_iota(jnp.int32, q.shape, 1)
    zero = jnp.zeros_like(q)
    q_rows = jnp.concatenate([jnp.where(lane < HEAD_DIM, q, zero),
                              jnp.where(lane >= HEAD_DIM, q, zero)], axis=0)
    cols = PAGE_SIZE * N_HEADS
    row_head = lax.broadcasted_iota(jnp.int32, (2 * N_HEADS, cols), 0) % N_HEADS
    col_head = lax.broadcasted_iota(jnp.int32, (2 * N_HEADS, cols), 1) % N_HEADS
    own_head = row_head == col_head

    @pl.when(p == 0)
    def _():
        kn = kn_ref[0]
        vn = vn_ref[0]
        s_new = jnp.sum(q_rows * jnp.concatenate([kn, kn], axis=0), axis=-1, keepdims=True)
        m_ref[...] = jnp.broadcast_to(s_new, m_ref.shape)
        l_ref[...] = jnp.ones_like(l_ref)
        acc_ref[...] = jnp.concatenate([vn, vn], axis=0)

    m = m_ref[:, 0:1]
    l = l_ref[:, 0:1]
    qb = q_rows.astype(BF16)
    scores = []
    for r in range(pages):
        kb = k_refs[r][0].reshape(cols, HEAD_W).astype(BF16)
        s = lax.dot_general(qb, kb, _NT, preferred_element_type=F32)
        scores.append(jnp.where(own_head, s, NEG))
    top = scores[0]
    for s in scores[1:]:
        top = jnp.maximum(top, s)
    m_new = jnp.maximum(m, jnp.max(top, axis=-1, keepdims=True))
    corr = jnp.exp2(m - m_new)
    l = l * corr
    acc = acc_ref[...] * corr
    for r in range(pages):
        pr = jnp.exp2(scores[r] - m_new)
        l = l + jnp.sum(pr, axis=-1, keepdims=True)
        vb = v_refs[r][0].reshape(cols, HEAD_W).astype(BF16)
        acc = acc + jnp.dot(pr.astype(BF16), vb, preferred_element_type=F32)
    m_ref[...] = jnp.broadcast_to(m_new, m_ref.shape)
    l_ref[...] = jnp.broadcast_to(l, l_ref.shape)
    acc_ref[...] = acc

    @pl.when(p == pl.num_programs(1) - 1)
    def _():
        o = acc_ref[...] / l_ref[...]
        w = o[:N_HEADS] - _diff_lambda(lam_ref) * o[N_HEADS:]
        o_ref[0] = (_rms(w, sg_ref[...]) * (1.0 - LAMBDA_INIT)).astype(o_ref.dtype)


def _attn_sample(page_table, lam4, subln_g, q, k_new, v_new, cache_k, cache_v, pages):
    bd, n_pages = page_table.shape
    page_blk = (1, PAGE_SIZE, N_HEADS, HEAD_W)

    def page_spec(r):
        return pl.BlockSpec(page_blk, lambda b, p, pt: (pt[b, p * pages + r], 0, 0, 0))

    tok = pl.BlockSpec((1, N_HEADS, HEAD_W), lambda b, p, pt: (b, 0, 0))
    const = lambda shape: pl.BlockSpec(shape, lambda b, p, pt: (0,) * len(shape))
    state = pltpu.VMEM((2 * N_HEADS, HEAD_W), F32)
    grid_spec = pltpu.PrefetchScalarGridSpec(
        num_scalar_prefetch=1,
        grid=(bd, n_pages // pages),
        in_specs=[const(lam4.shape), const((1, HEAD_W)), tok, tok, tok]
                 + [page_spec(r) for r in range(pages)]
                 + [page_spec(r) for r in range(pages)],
        out_specs=tok,
        scratch_shapes=[state, state, state],
    )
    return pl.pallas_call(
        functools.partial(_attn_sample_kernel, pages=pages),
        grid_spec=grid_spec,
        out_shape=jax.ShapeDtypeStruct((bd, N_HEADS, HEAD_W), BF16),
        compiler_params=_params("arbitrary", "arbitrary"),
        name="attn_sample",
    )(page_table, lam4, subln_g, q, k_new, v_new,
      *([cache_k] * pages), *([cache_v] * pages))


CONV_HALO = 32
CONV_ROWS = 64


def _conv_finish(y, g_ref):
    c = _rms(y, g_ref[...])
    return (c * _sigmoid(c)).astype(BF16)


def _conv_prompt_kernel(u_ref, halo_ref, w_ref, b_ref, g_ref, o_ref, win_ref, y_ref, *, ts):
    i = pl.program_id(1)
    halo = halo_ref[0]
    win_ref[0:CONV_HALO, :] = jnp.where(i == 0, jnp.zeros_like(halo), halo)
    win_ref[CONV_HALO:, :] = u_ref[0]
    shift = CONV_HALO - CONV_STATE

    def lane_chunk(c, _):
        lanes = pl.ds(pl.multiple_of(c * LANES, LANES), LANES)
        bias = b_ref[:, lanes]
        for r in range(ts // CONV_ROWS):
            acc = jnp.zeros((CONV_ROWS, LANES), F32) + bias
            for j in range(CONV_WIDTH):
                row0 = r * CONV_ROWS + j + shift
                acc = acc + w_ref[j:j + 1, lanes] * win_ref[row0:row0 + CONV_ROWS, lanes]
            y_ref[r * CONV_ROWS:(r + 1) * CONV_ROWS, lanes] = acc
        return 0

    lax.fori_loop(0, CONV_W // LANES, lane_chunk, 0)
    o_ref[0] = _conv_finish(y_ref[...], g_ref)


def _conv_prompt(u, w_dw, b_dw, g, ts):
    b, s, _ = u.shape
    per = ts // CONV_HALO
    return pl.pallas_call(
        functools.partial(_conv_prompt_kernel, ts=ts),
        grid=(b, s // ts),
        in_specs=[pl.BlockSpec((1, ts, CONV_W), lambda bi, i: (bi, i, 0)),
                  pl.BlockSpec((1, CONV_HALO, CONV_W),
                               lambda bi, i: (bi, jnp.maximum(i * per - 1, 0), 0)),
                  _resident(w_dw.shape),
                  _resident((1, CONV_W)),
                  _resident((1, CONV_W))],
        out_specs=pl.BlockSpec((1, ts, CONV_W), lambda bi, i: (bi, i, 0)),
        out_shape=jax.ShapeDtypeStruct((b, s, CONV_W), BF16),
        scratch_shapes=[pltpu.VMEM((CONV_HALO + ts, CONV_W), F32),
                        pltpu.VMEM((ts, CONV_W), F32)],
        compiler_params=_params("arbitrary", "arbitrary"),
        name="conv_prompt",
    )(u, u, w_dw, b_dw, g)


def _conv_sample_kernel(st_ref, u_ref, w_ref, b_ref, g_ref, o_ref):
    acc = b_ref[...] + w_ref[CONV_STATE:CONV_WIDTH, :] * u_ref[...]
    for j in range(CONV_STATE):
        acc = acc + w_ref[j:j + 1, :] * st_ref[:, j, :]
    o_ref[...] = _conv_finish(acc, g_ref)


def _conv_sample(state, u, w_dw, b_dw, g):
    bd = u.shape[0]
    return pl.pallas_call(
        _conv_sample_kernel,
        out_shape=jax.ShapeDtypeStruct((bd, CONV_W), BF16),
        compiler_params=pltpu.CompilerParams(vmem_limit_bytes=VMEM_LIMIT_BYTES),
        name="conv_sample",
    )(state, u, w_dw, b_dw, g)


def _out_proj_kernel(x_ref, a_ref, c_ref, wo_ref, g_ref, wq_ref, x1_ref, xnt_ref, qp_ref):
    x1 = (x_ref[...]
          + jnp.dot(a_ref[...], wo_ref[0:ATTN_W, :], preferred_element_type=F32)
          + jnp.dot(c_ref[...], wo_ref[ATTN_W:D_MODEL, :], preferred_element_type=F32))
    x1_ref[...] = x1
    xn = _rms(x1, g_ref[...])
    xnt_ref[...] = xn.T.astype(BF16)
    qp_ref[...] = jnp.dot(xn.astype(BF16), wq_ref[...],
                          preferred_element_type=F32).astype(BF16)


def _out_proj(x2d, a, c, w_out_b, g, w_query_b, tm):
    t = x2d.shape[0]
    row = lambda i: (i, 0)
    return pl.pallas_call(
        _out_proj_kernel,
        grid=(t // tm,),
        in_specs=[pl.BlockSpec((tm, D_MODEL), row),
                  pl.BlockSpec((tm, ATTN_W), row),
                  pl.BlockSpec((tm, CONV_W), row),
                  _resident(w_out_b.shape),
                  _resident((1, D_MODEL)),
                  _resident(w_query_b.shape)],
        out_specs=[pl.BlockSpec((tm, D_MODEL), row),
                   pl.BlockSpec((D_MODEL, tm), lambda i: (0, i)),
                   pl.BlockSpec((tm, D_MODEL), row)],
        out_shape=[jax.ShapeDtypeStruct((t, D_MODEL), F32),
                   jax.ShapeDtypeStruct((D_MODEL, t), BF16),
                   jax.ShapeDtypeStruct((t, D_MODEL), BF16)],
        compiler_params=_params("arbitrary"),
        name="out_proj",
    )(x2d, a, c, w_out_b, g, w_query_b)


_PAIR_WIDTHS = [P_TOPK // (a + 1) for a in range(P_TOPK)]
N_PAIRS = sum(_PAIR_WIDTHS)
PAIR_ROWS = -(-N_PAIRS // SUBLANES) * SUBLANES


def _topk_ranks(x, stable):
    rows = x.shape[0]
    rank = jnp.full(x.shape, float(P_TOPK), F32)
    if stable:
        row = lax.broadcasted_iota(jnp.int32, x.shape, 0).astype(F32)
    vals = []
    for r in range(P_TOPK):
        m = jnp.max(x, axis=0, keepdims=True)
        if stable:
            first = jnp.min(jnp.where(x == m, row, float(rows)), axis=0, keepdims=True)
            hit = row == first
        else:
            hit = x == m
        rank = jnp.where(hit, float(r), rank)
        x = jnp.where(hit, -jnp.inf, x)
        vals.append(m)
    ranked = jnp.sum(jnp.where(rank < float(P_TOPK), 1.0, 0.0), axis=0, keepdims=True)
    return rank, vals, ranked


def _peer_select_kernel(qp_ref, sk_ref, r2_ref, e2_ref, n1_ref, e1_ref,
                        t2_ref, cand_ref, pick_ref):
    tm = qp_ref.shape[0]
    cand_ref[N_PAIRS:, :] = jnp.full((PAIR_ROWS - N_PAIRS, tm), -jnp.inf, F32)

    def head(h, stable):
        def score(c):
            q = qp_ref[:, (2 * h + c) * N_KEYS:(2 * h + c + 1) * N_KEYS]
            return lax.dot_general(sk_ref[h, c], q, _NT, preferred_element_type=F32)

        s1 = score(0)
        s2 = score(1)
        rank1, top1, ranked1 = _topk_ranks(s1, stable)
        rank2, top2, ranked2 = _topk_ranks(s2, stable)
        for r in range(P_TOPK):
            t2_ref[r:r + 1, :] = top2[r]
        pos = 0
        for a, width in enumerate(_PAIR_WIDTHS):
            cand_ref[pos:pos + width, :] = top1[a] + t2_ref[0:width, :]
            pos += width
        cand = cand_ref[...]
        crank, _, ranked3 = _topk_ranks(cand, stable)
        chosen = crank < float(P_TOPK)
        best = top1[0] + top2[0]
        z = jnp.sum(jnp.where(chosen, jnp.exp(cand - best), 0.0), axis=0, keepdims=True)
        pick_ref[...] = jnp.where(chosen, 1.0, 0.0)
        n1 = jnp.zeros_like(s1)
        pos = 0
        for a, width in enumerate(_PAIR_WIDTHS):
            count = jnp.sum(pick_ref[pos:pos + width, :], axis=0, keepdims=True)
            n1 = jnp.where(rank1 == float(a), count, n1)
            pos += width
        r2_ref[h] = rank2.astype(BF16)
        e2_ref[h] = jnp.exp(s2 - top2[0]).astype(BF16)
        n1_ref[h] = n1
        e1_ref[h] = jnp.exp(s1 - top1[0]) / z
        k = float(P_TOPK)
        return jnp.abs(ranked1 - k) + jnp.abs(ranked2 - k) + jnp.abs(ranked3 - k)

    for h in range(P_HEADS):
        ties = head(h, False)

        @pl.when(jnp.max(ties) > 0.0)
        def _():
            head(h, True)


def _peer_select(qp, sub_keys_b, tm):
    t = qp.shape[0]
    blk = pl.BlockSpec((P_HEADS, N_KEYS, tm), lambda i: (0, 0, i))
    out = lambda dt: jax.ShapeDtypeStruct((P_HEADS, N_KEYS, t), dt)
    return pl.pallas_call(
        _peer_select_kernel,
        grid=(t // tm,),
        in_specs=[pl.BlockSpec((tm, D_MODEL), lambda i: (i, 0)),
                  _resident(sub_keys_b.shape)],
        out_specs=[blk] * 4,
        out_shape=[out(BF16), out(BF16), out(F32), out(F32)],
        scratch_shapes=[pltpu.VMEM((P_TOPK, tm), F32),
                        pltpu.VMEM((PAIR_ROWS, tm), F32),
                        pltpu.VMEM((PAIR_ROWS, tm), F32)],
        compiler_params=_params("arbitrary"),
        name="peer_select",
    )(qp, sub_keys_b)


def _gelu(h):
    return 0.5 * h * (1.0 + lax.erf(h * (2.0 ** -0.5)))


BF16_ROWS = 2 * SUBLANES


DENSE_BLOCK = 2 * N_KEYS


def _peer_dense_kernel(xnt_ref, x1_ref, u_ref, v_ref, r2_ref, e2_ref, n1_ref, e1_ref,
                       o_ref, *, te):
    j = pl.program_id(1)
    tm = xnt_ref.shape[1]
    zero = jnp.zeros((), BF16)

    @pl.when(j == 0)
    def _():
        o_ref[...] = x1_ref[...]

    def token_row(ref, h, i1):
        row = jnp.broadcast_to(ref[h, pl.ds(i1, 1), :], (BF16_ROWS, tm))
        return row.astype(BF16)[None]

    def hidden(b, part):
        rows = slice(b * DENSE_BLOCK, (b + 1) * DENSE_BLOCK)
        cols = slice(part * (tm // n_parts), (part + 1) * (tm // n_parts))
        return jnp.dot(u_ref[rows, :], xnt_ref[:, cols], preferred_element_type=F32)

    n_blocks = te // DENSE_BLOCK
    n_parts = DENSE_BLOCK // N_KEYS
    ht_next = [hidden(0, part) for part in range(n_parts)]
    for b in range(n_blocks):
        rows = slice(b * DENSE_BLOCK, (b + 1) * DENSE_BLOCK)
        ht = jnp.concatenate(ht_next, axis=-1)
        ht_next = []
        parts = []
        for kb in range(n_parts):
            if b + 1 < n_blocks:
                ht_next.append(hidden(b + 1, kb))
            i1 = j * (te // N_KEYS) + b * n_parts + kb
            gate = None
            for h in range(P_HEADS):
                term = token_row(e1_ref, h, i1) * jnp.where(
                    r2_ref[h] < token_row(n1_ref, h, i1), e2_ref[h], zero)
                gate = term if gate is None else gate + term
            act = _gelu(ht[kb * N_KEYS:(kb + 1) * N_KEYS]).astype(BF16).reshape(gate.shape)
            parts.append((act * gate).reshape(N_KEYS, tm))
        at = jnp.concatenate(parts, axis=0)
        o_ref[...] += lax.dot_general(at, v_ref[rows, :], _TN, preferred_element_type=F32)


def _peer_dense(xnt, x1, eu_b, ev_b, r2, e2, n1, e1, tm, te):
    t = x1.shape[0]
    n_exp = eu_b.shape[0]
    once = pl.Buffered(1)
    tok = lambda i, j: (i, 0)
    exp = pl.BlockSpec((te, D_MODEL), lambda i, j: (j, 0))
    packed = pl.BlockSpec((P_HEADS, N_KEYS // BF16_ROWS, BF16_ROWS, tm),
                          lambda i, j: (0, 0, 0, i), pipeline_mode=once)
    rows = pl.BlockSpec((P_HEADS, N_KEYS, tm), lambda i, j: (0, 0, i), pipeline_mode=once)
    split = lambda y: y.reshape(P_HEADS, N_KEYS // BF16_ROWS, BF16_ROWS, t)
    return pl.pallas_call(
        functools.partial(_peer_dense_kernel, te=te),
        grid=(t // tm, n_exp // te),
        in_specs=[pl.BlockSpec((D_MODEL, tm), lambda i, j: (0, i), pipeline_mode=once),
                  pl.BlockSpec((tm, D_MODEL), tok, pipeline_mode=once),
                  exp, exp, packed, packed, rows, rows],
        out_specs=pl.BlockSpec((tm, D_MODEL), tok),
        out_shape=jax.ShapeDtypeStruct((t, D_MODEL), F32),
        compiler_params=_params("arbitrary", "arbitrary"),
        name="peer_dense",
    )(xnt, x1, eu_b, ev_b, split(r2), split(e2), n1, e1)


def _tiles():
    return dict(proj_rows=256, attn_q=512, conv_rows=256, select_tokens=128,
                dense_tokens=512, dense_experts=1024, sample_pages=8, sample_rows=128)


def _block_ones(width, group):
    r = jnp.arange(width) // group
    return (r[:, None] == r[None, :]).astype(BF16)


def _finish(x2d, a, c, w_out_b, norm_ffn_g, w_query_b, sub_keys_b, eu_b, ev_b, tiles, tm):
    x1, xnt, qp = _out_proj(x2d, a, c, w_out_b, norm_ffn_g, w_query_b, tm)
    sel = _peer_select(qp, sub_keys_b, min(tiles["select_tokens"], x2d.shape[0]))
    return _peer_dense(xnt, x1, eu_b, ev_b, *sel,
                       min(tiles["dense_tokens"], x2d.shape[0]), tiles["dense_experts"])


def kernel(x_prompt, x_sample, cache_k, cache_v, state_conv, page_table, norm_mix_g, w_in,
           q_norm_g, k_norm_g, lambda_q1, lambda_k1, lambda_q2, lambda_k2, subln_g, w_dw, b_dw,
           conv_norm_g, w_out, norm_ffn_g, w_query, sub_keys, expert_u, expert_v):
    tiles = _tiles()
    b, s, _ = x_prompt.shape
    bd = x_sample.shape[0]
    assert x_sample.shape[1] == 1

    w_in_b = w_in.astype(BF16)
    w_out_b = w_out.astype(BF16)
    w_query_b = w_query.astype(BF16)
    sub_keys_b = sub_keys.astype(BF16)
    eu_b = expert_u.astype(BF16)
    ev_b = expert_v.astype(BF16)
    row = lambda v: v.reshape(1, -1)
    qg = row(jnp.tile(q_norm_g, ATTN_W // HEAD_DIM))
    kg = row(jnp.tile(k_norm_g, ATTN_W // HEAD_DIM))
    lam4 = jnp.stack([lambda_q1, lambda_k1, lambda_q2, lambda_k2])
    pmat = _block_ones(2 * LANES, HEAD_DIM)
    sg, bdw, cg = row(subln_g), row(b_dw), row(conv_norm_g)
    nmix, nffn = row(norm_mix_g), row(norm_ffn_g)

    xp = x_prompt.reshape(b * s, D_MODEL)
    qb, k, kb, v, vb, u = _in_proj(xp, nmix, w_in_b, qg, kg, pmat, tiles["proj_rows"])
    r3 = lambda y: y.reshape(b, s, -1)
    a_p = _attn_prompt(lam4, sg, r3(qb), r3(kb), r3(vb), tiles["attn_q"])
    c_p = _conv_prompt(r3(u), w_dw, bdw, cg, tiles["conv_rows"])
    y_p = _finish(xp, a_p.reshape(b * s, ATTN_W), c_p.reshape(b * s, CONV_W), w_out_b, nffn,
                  w_query_b, sub_keys_b, eu_b, ev_b, tiles, tiles["proj_rows"])

    rows = tiles["sample_rows"]
    xs = jnp.pad(x_sample.reshape(bd, D_MODEL), ((0, rows - bd), (0, 0)))
    qs, ks, _, vs, _, us = _in_proj(xs, nmix, w_in_b, qg, kg, pmat, rows)
    ks, vs, us = ks[:bd], vs[:bd], us[:bd]
    h3 = lambda y: y.reshape(bd, N_HEADS, HEAD_W)
    a_s = _attn_sample(page_table, lam4, sg, h3(qs[:bd].astype(F32)), h3(ks), h3(vs),
                       cache_k, cache_v, tiles["sample_pages"])
    c_s = _conv_sample(state_conv, us, w_dw, bdw, cg)
    pad = lambda y: jnp.pad(y, ((0, rows - bd), (0, 0)))
    y_s = _finish(xs, pad(a_s.reshape(bd, ATTN_W)), pad(c_s), w_out_b, nffn,
                  w_query_b, sub_keys_b, eu_b, ev_b, tiles, rows)[:bd]

    heads = lambda y, n: y.reshape(n, -1, N_HEADS, HEAD_W)
    conv_prompt = r3(u)[:, s - CONV_STATE:]
    conv_sample = jnp.concatenate([state_conv[:, 1:], us[:, None, :]], axis=1)
    return (y_p.reshape(b, s, D_MODEL), y_s.reshape(bd, 1, D_MODEL),
            heads(k, b), heads(v, b), conv_prompt,
            heads(ks, bd), heads(vs, bd), conv_sample)
```

```python
import functools
import math

import jax
import jax.numpy as jnp
from jax import lax
from jax.experimental import pallas as pl
from jax.experimental.pallas import tpu as pltpu

F32 = jnp.float32
BF16 = jnp.bfloat16

D_MODEL = 2048
N_HEADS = 8
HEAD_DIM = 64
HEAD_W = 2 * HEAD_DIM
ATTN_W = N_HEADS * HEAD_W
CONV_W = D_MODEL - ATTN_W
CONV_WIDTH = 31
CONV_STATE = CONV_WIDTH - 1
LAMBDA_INIT = 0.8 - 0.6 * math.exp(-0.3 * (1 - 1))
SCALE = HEAD_DIM ** -0.5
PAGE_SIZE = 128
N_KEYS = 128
P_HEADS = 8
P_TOPK = 16
EPS = 1e-6
NEG = -1e30
LOG2E = math.log2(math.e)

SUBLANES = 8
LANES = 128
VMEM_LIMIT_BYTES = 56 * 1024 * 1024

_NT = (((1,), (1,)), ((), ()))
_TN = (((0,), (0,)), ((), ()))


def _params(*sem):
    return pltpu.CompilerParams(dimension_semantics=sem, vmem_limit_bytes=VMEM_LIMIT_BYTES)


def _resident(shape):
    return pl.BlockSpec(shape, lambda *_: (0,) * len(shape), pipeline_mode=pl.Buffered(1))


def _rms(x, gain):
    ms = jnp.mean(x * x, axis=-1, keepdims=True)
    return x * lax.rsqrt(ms + EPS) * gain


def _sigmoid(x):
    return 1.0 / (1.0 + jnp.exp(-x))


def _diff_lambda(lam_ref):
    l = lam_ref[...]
    e1 = jnp.exp(jnp.sum(l[0:1] * l[1:2], axis=-1, keepdims=True))
    e2 = jnp.exp(jnp.sum(l[2:3] * l[3:4], axis=-1, keepdims=True))
    return e1 - e2 + LAMBDA_INIT


def _in_proj_kernel(x_ref, g_ref, w_ref, qg_ref, kg_ref, p_ref,
                    qb_ref, k_ref, kb_ref, v_ref, vb_ref, u_ref):
    xn = _rms(x_ref[...], g_ref[...]).astype(BF16)

    def proj(col):
        return jnp.dot(xn, w_ref[:, col:col + ATTN_W], preferred_element_type=F32)

    def head_norm(y, gain):
        y2 = y * y
        hi = y2.astype(BF16)
        lo = (y2 - hi.astype(F32)).astype(BF16)
        p = p_ref[...]
        w = p.shape[0]
        parts = []
        for c in range(ATTN_W // w):
            sl = slice(c * w, (c + 1) * w)
            parts.append(jnp.dot(hi[:, sl], p, preferred_element_type=F32)
                         + jnp.dot(lo[:, sl], p, preferred_element_type=F32))
        ss = jnp.concatenate(parts, axis=-1)
        return y * lax.rsqrt(ss * (1.0 / HEAD_DIM) + EPS) * gain

    q = head_norm(proj(0), qg_ref[...])
    qb_ref[...] = (q * (SCALE * LOG2E)).astype(BF16)
    k = head_norm(proj(ATTN_W), kg_ref[...])
    k_ref[...] = k
    kb_ref[...] = k.astype(BF16)
    v = proj(2 * ATTN_W)
    v_ref[...] = v
    vb_ref[...] = v.astype(BF16)
    a = proj(3 * ATTN_W)
    gt = proj(3 * ATTN_W + CONV_W)
    u_ref[...] = a * _sigmoid(gt)


def _in_proj(x2d, norm_g, w_in_b, qg, kg, pmat, tm):
    t = x2d.shape[0]
    row = lambda i: (i, 0)
    blk = lambda: pl.BlockSpec((tm, ATTN_W), row)
    return pl.pallas_call(
        _in_proj_kernel,
        grid=(t // tm,),
        in_specs=[pl.BlockSpec((tm, D_MODEL), row),
                  _resident((1, D_MODEL)),
                  _resident(w_in_b.shape),
                  _resident((1, ATTN_W)),
                  _resident((1, ATTN_W)),
                  _resident(pmat.shape)],
        out_specs=[blk(), blk(), blk(), blk(), blk(), blk()],
        out_shape=[jax.ShapeDtypeStruct((t, ATTN_W), BF16),
                   jax.ShapeDtypeStruct((t, ATTN_W), F32),
                   jax.ShapeDtypeStruct((t, ATTN_W), BF16),
                   jax.ShapeDtypeStruct((t, ATTN_W), F32),
                   jax.ShapeDtypeStruct((t, ATTN_W), BF16),
                   jax.ShapeDtypeStruct((t, CONV_W), F32)],
        compiler_params=_params("arbitrary"),
        name="in_proj",
    )(x2d, norm_g, w_in_b, qg, kg, pmat)


def _lane_fold(x, op):
    out = x[:, :LANES]
    for c in range(1, x.shape[1] // LANES):
        out = op(out, x[:, c * LANES:(c + 1) * LANES])
    return out


def _attn_prompt_kernel(lam_ref, sg_ref, q_ref, k_ref, v_ref, o_ref, s1_ref, s2_ref, *, tq):
    lam = _diff_lambda(lam_ref)
    seq = q_ref.shape[1]
    lane = lax.broadcasted_iota(jnp.int32, (tq, HEAD_W), 1)
    r = lax.broadcasted_iota(jnp.int32, (tq, tq), 0)
    c = lax.broadcasted_iota(jnp.int32, (tq, tq), 1)
    keep = c <= r
    for i in range(seq // tq):
        rows = slice(i * tq, (i + 1) * tq)
        q = q_ref[0, rows, :]
        zero = jnp.zeros_like(q)
        qs = (jnp.where(lane < HEAD_DIM, q, zero), jnp.where(lane >= HEAD_DIM, q, zero))
        tops = [jnp.full((tq, LANES), NEG, F32) for _ in qs]
        for j in range(i + 1):
            cols = slice(j * tq, (j + 1) * tq)
            kj = k_ref[0, cols, :]
            for n, s_ref in enumerate((s1_ref, s2_ref)):
                s = lax.dot_general(qs[n], kj, _NT, preferred_element_type=F32)
                if j == i:
                    s = jnp.where(keep, s, NEG)
                s_ref[:, cols] = s
                tops[n] = jnp.maximum(tops[n], _lane_fold(s, jnp.maximum))
        outs = []
        for n, s_ref in enumerate((s1_ref, s2_ref)):
            m = jnp.max(tops[n], axis=-1, keepdims=True)
            part = jnp.zeros((tq, LANES), F32)
            acc = jnp.zeros((tq, HEAD_W), F32)
            for j in range(i + 1):
                cols = slice(j * tq, (j + 1) * tq)
                p = jnp.exp2(s_ref[:, cols] - m)
                part = part + _lane_fold(p, jnp.add)
                acc = acc + jnp.dot(p.astype(BF16), v_ref[0, cols, :],
                                    preferred_element_type=F32)
            outs.append(acc / jnp.sum(part, axis=-1, keepdims=True))
        o = outs[0] - lam * outs[1]
        o_ref[0, rows, :] = (_rms(o, sg_ref[...]) * (1.0 - LAMBDA_INIT)).astype(o_ref.dtype)


def _attn_prompt(lam4, subln_g, qb, kb, vb, tq):
    b, s, _ = qb.shape
    head = pl.BlockSpec((1, s, HEAD_W), lambda bi, h: (bi, 0, h))
    return pl.pallas_call(
        functools.partial(_attn_prompt_kernel, tq=tq),
        grid=(b, N_HEADS),
        in_specs=[_resident(lam4.shape), _resident((1, HEAD_W)), head, head, head],
        out_specs=head,
        out_shape=jax.ShapeDtypeStruct((b, s, ATTN_W), BF16),
        scratch_shapes=[pltpu.VMEM((tq, s), F32), pltpu.VMEM((tq, s), F32)],
        compiler_params=_params("arbitrary", "arbitrary"),
        name="attn_prompt",
    )(lam4, subln_g, qb, kb, vb)


def _attn_sample_kernel(pt_ref, lam_ref, sg_ref, q_ref, kn_ref, vn_ref, *rest, pages):
    k_refs = rest[:pages]
    v_refs = rest[pages:2 * pages]
    o_ref = rest[2 * pages]
    m_ref, l_ref, acc_ref = rest[2 * pages + 1:]
    p = pl.program_id(1)
    q = q_ref[0]
    lane = lax.broadcasted_iota(jnp.int32, q.shape, 1)
    zero = jnp.zeros_like(q)
    q_rows = jnp.concatenate([jnp.where(lane < HEAD_DIM, q, zero),
                              jnp.where(lane >= HEAD_DIM, q, zero)], axis=0)
    cols = PAGE_SIZE * N_HEADS
    row_head = lax.broadcasted_iota(jnp.int32, (2 * N_HEADS, cols), 0) % N_HEADS
    col_head = lax.broadcasted_iota(jnp.int32, (2 * N_HEADS, cols), 1) % N_HEADS
    own_head = row_head == col_head

    @pl.when(p == 0)
    def _():
        kn = kn_ref[0]
        vn = vn_ref[0]
        s_new = jnp.sum(q_rows * jnp.concatenate([kn, kn], axis=0), axis=-1, keepdims=True)
        m_ref[...] = jnp.broadcast_to(s_new, m_ref.shape)
        l_ref[...] = jnp.ones_like(l_ref)
        acc_ref[...] = jnp.concatenate([vn, vn], axis=0)

    m = m_ref[:, 0:1]
    l = l_ref[:, 0:1]
    qb = q_rows.astype(BF16)
    scores = []
    for r in range(pages):
        kb = k_refs[r][0].reshape(cols, HEAD_W).astype(BF16)
        s = lax.dot_general(qb, kb, _NT, preferred_element_type=F32)
        scores.append(jnp.where(own_head, s, NEG))
    top = scores[0]
    for s in scores[1:]:
        top = jnp.maximum(top, s)
    m_new = jnp.maximum(m, jnp.max(top, axis=-1, keepdims=True))
    corr = jnp.exp2(m - m_new)
    l = l * corr
    acc = acc_ref[...] * corr
    for r in range(pages):
        pr = jnp.exp2(scores[r] - m_new)
        l = l + jnp.sum(pr, axis=-1, keepdims=True)
        vb = v_refs[r][0].reshape(cols, HEAD_W).astype(BF16)
        acc = acc + jnp.dot(pr.astype(BF16), vb, preferred_element_type=F32)
    m_ref[...] = jnp.broadcast_to(m_new, m_ref.shape)
    l_ref[...] = jnp.broadcast_to(l, l_ref.shape)
    acc_ref[...] = acc

    @pl.when(p == pl.num_programs(1) - 1)
    def _():
        o = acc_ref[...] / l_ref[...]
        w = o[:N_HEADS] - _diff_lambda(lam_ref) * o[N_HEADS:]
        o_ref[0] = (_rms(w, sg_ref[...]) * (1.0 - LAMBDA_INIT)).astype(o_ref.dtype)


def _attn_sample(page_table, lam4, subln_g, q, k_new, v_new, cache_k, cache_v, pages):
    bd, n_pages = page_table.shape
    page_blk = (1, PAGE_SIZE, N_HEADS, HEAD_W)

    def page_spec(r):
        return pl.BlockSpec(page_blk, lambda b, p, pt: (pt[b, p * pages + r], 0, 0, 0))

    tok = pl.BlockSpec((1, N_HEADS, HEAD_W), lambda b, p, pt: (b, 0, 0))
    const = lambda shape: pl.BlockSpec(shape, lambda b, p, pt: (0,) * len(shape))
    state = pltpu.VMEM((2 * N_HEADS, HEAD_W), F32)
    grid_spec = pltpu.PrefetchScalarGridSpec(
        num_scalar_prefetch=1,
        grid=(bd, n_pages // pages),
        in_specs=[const(lam4.shape), const((1, HEAD_W)), tok, tok, tok]
                 + [page_spec(r) for r in range(pages)]
                 + [page_spec(r) for r in range(pages)],
        out_specs=tok,
        scratch_shapes=[state, state, state],
    )
    return pl.pallas_call(
        functools.partial(_attn_sample_kernel, pages=pages),
        grid_spec=grid_spec,
        out_shape=jax.ShapeDtypeStruct((bd, N_HEADS, HEAD_W), BF16),
        compiler_params=_params("arbitrary", "arbitrary"),
        name="attn_sample",
    )(page_table, lam4, subln_g, q, k_new, v_new,
      *([cache_k] * pages), *([cache_v] * pages))


CONV_HALO = 32
CONV_ROWS = 64


def _conv_finish(y, g_ref):
    c = _rms(y, g_ref[...])
    return (c * _sigmoid(c)).astype(BF16)


def _conv_prompt_kernel(u_ref, halo_ref, w_ref, b_ref, g_ref, o_ref, win_ref, sh_ref, y_ref,
                        *, ts):
    i = pl.program_id(1)
    halo = halo_ref[0]
    win_ref[0:CONV_HALO, :] = jnp.where(i == 0, jnp.zeros_like(halo), halo)
    win_ref[CONV_HALO:, :] = u_ref[0]
    shift = CONV_HALO - CONV_STATE
    span = sh_ref.shape[1]

    def lane_chunk(c, _):
        lanes = pl.ds(pl.multiple_of(c * LANES, LANES), LANES)
        for phase in range(1, SUBLANES):
            sh_ref[phase] = win_ref[phase:phase + span, lanes]
        bias = b_ref[:, lanes]
        for r in range(ts // CONV_ROWS):
            acc = jnp.zeros((CONV_ROWS, LANES), F32) + bias
            for j in range(CONV_WIDTH):
                phase = (j + shift) % SUBLANES
                row0 = r * CONV_ROWS + j + shift - phase
                if phase == 0:
                    rows = win_ref[row0:row0 + CONV_ROWS, lanes]
                else:
                    rows = sh_ref[phase, row0:row0 + CONV_ROWS, :]
                acc = acc + w_ref[j:j + 1, lanes] * rows
            y_ref[r * CONV_ROWS:(r + 1) * CONV_ROWS, lanes] = acc
        return 0

    lax.fori_loop(0, CONV_W // LANES, lane_chunk, 0)
    o_ref[0] = _conv_finish(y_ref[...], g_ref)


def _conv_prompt(u, w_dw, b_dw, g, ts):
    b, s, _ = u.shape
    per = ts // CONV_HALO
    return pl.pallas_call(
        functools.partial(_conv_prompt_kernel, ts=ts),
        grid=(b, s // ts),
        in_specs=[pl.BlockSpec((1, ts, CONV_W), lambda bi, i: (bi, i, 0)),
                  pl.BlockSpec((1, CONV_HALO, CONV_W),
                               lambda bi, i: (bi, jnp.maximum(i * per - 1, 0), 0)),
                  _resident(w_dw.shape),
                  _resident((1, CONV_W)),
                  _resident((1, CONV_W))],
        out_specs=pl.BlockSpec((1, ts, CONV_W), lambda bi, i: (bi, i, 0)),
        out_shape=jax.ShapeDtypeStruct((b, s, CONV_W), BF16),
        scratch_shapes=[pltpu.VMEM((CONV_HALO + ts, CONV_W), F32),
                        pltpu.VMEM((SUBLANES, CONV_HALO + ts - SUBLANES, LANES), F32),
                        pltpu.VMEM((ts, CONV_W), F32)],
        compiler_params=_params("arbitrary", "arbitrary"),
        name="conv_prompt",
    )(u, u, w_dw, b_dw, g)


def _conv_sample_kernel(st_ref, u_ref, w_ref, b_ref, g_ref, o_ref):
    acc = b_ref[...] + w_ref[CONV_STATE:CONV_WIDTH, :] * u_ref[...]
    for j in range(CONV_STATE):
        acc = acc + w_ref[j:j + 1, :] * st_ref[:, j, :]
    o_ref[...] = _conv_finish(acc, g_ref)


def _conv_sample(state, u, w_dw, b_dw, g):
    bd = u.shape[0]
    return pl.pallas_call(
        _conv_sample_kernel,
        out_shape=jax.ShapeDtypeStruct((bd, CONV_W), BF16),
        compiler_params=pltpu.CompilerParams(vmem_limit_bytes=VMEM_LIMIT_BYTES),
        name="conv_sample",
    )(state, u, w_dw, b_dw, g)


def _out_proj_kernel(x_ref, a_ref, c_ref, wo_ref, g_ref, wq_ref, x1_ref, xnt_ref, qp_ref):
    x1 = (x_ref[...]
          + jnp.dot(a_ref[...], wo_ref[0:ATTN_W, :], preferred_element_type=F32)
          + jnp.dot(c_ref[...], wo_ref[ATTN_W:D_MODEL, :], preferred_element_type=F32))
    x1_ref[...] = x1
    xn = _rms(x1, g_ref[...])
    xnt_ref[...] = xn.T.astype(BF16)
    qp_ref[...] = jnp.dot(xn.astype(BF16), wq_ref[...],
                          preferred_element_type=F32).astype(BF16)


def _out_proj(x2d, a, c, w_out_b, g, w_query_b, tm):
    t = x2d.shape[0]
    row = lambda i: (i, 0)
    return pl.pallas_call(
        _out_proj_kernel,
        grid=(t // tm,),
        in_specs=[pl.BlockSpec((tm, D_MODEL), row),
                  pl.BlockSpec((tm, ATTN_W), row),
                  pl.BlockSpec((tm, CONV_W), row),
                  _resident(w_out_b.shape),
                  _resident((1, D_MODEL)),
                  _resident(w_query_b.shape)],
        out_specs=[pl.BlockSpec((tm, D_MODEL), row),
                   pl.BlockSpec((D_MODEL, tm), lambda i: (0, i)),
                   pl.BlockSpec((tm, D_MODEL), row)],
        out_shape=[jax.ShapeDtypeStruct((t, D_MODEL), F32),
                   jax.ShapeDtypeStruct((D_MODEL, t), BF16),
                   jax.ShapeDtypeStruct((t, D_MODEL), BF16)],
        compiler_params=_params("arbitrary"),
        name="out_proj",
    )(x2d, a, c, w_out_b, g, w_query_b)


_PAIR_WIDTHS = [P_TOPK // (a + 1) for a in range(P_TOPK)]
N_PAIRS = sum(_PAIR_WIDTHS)
PAIR_ROWS = -(-N_PAIRS // SUBLANES) * SUBLANES


def _topk_ranks(xs, stable):
    xs = list(xs)
    ranks = [jnp.full(x.shape, float(P_TOPK), F32) for x in xs]
    vals = [[] for _ in xs]
    if stable:
        rows = [lax.broadcasted_iota(jnp.int32, x.shape, 0).astype(F32) for x in xs]
    for r in range(P_TOPK):
        for a, x in enumerate(xs):
            m = jnp.max(x, axis=0, keepdims=True)
            if stable:
                first = jnp.min(jnp.where(x == m, rows[a], float(x.shape[0])),
                                axis=0, keepdims=True)
                hit = rows[a] == first
            else:
                hit = x == m
            ranks[a] = jnp.where(hit, float(r), ranks[a])
            xs[a] = jnp.where(hit, -jnp.inf, x)
            vals[a].append(m)
    ranked = [jnp.sum(jnp.where(rk < float(P_TOPK), 1.0, 0.0), axis=0, keepdims=True)
              for rk in ranks]
    return list(zip(ranks, vals, ranked))


def _peer_select_kernel(qp_ref, sk_ref, r2_ref, e2_ref, n1_ref, e1_ref,
                        t2_ref, cand_ref, pick_ref):
    tm = qp_ref.shape[0]
    k = float(P_TOPK)
    cand_ref[N_PAIRS:, :] = jnp.full((PAIR_ROWS - N_PAIRS, tm), -jnp.inf, F32)

    def scores(h):
        def score(c):
            q = qp_ref[:, (2 * h + c) * N_KEYS:(2 * h + c + 1) * N_KEYS]
            return lax.dot_general(sk_ref[h, c], q, _NT, preferred_element_type=F32)
        return score(0), score(1)

    def pair_sums(top1, top2):
        for r in range(P_TOPK):
            t2_ref[r:r + 1, :] = top2[r]
        pos = 0
        for a, width in enumerate(_PAIR_WIDTHS):
            cand_ref[pos:pos + width, :] = top1[a] + t2_ref[0:width, :]
            pos += width
        return cand_ref[...]

    def finish(h, s1, s2, rank1, top1, rank2, top2, cand, crank):
        chosen = crank < k
        best = top1[0] + top2[0]
        z = jnp.sum(jnp.where(chosen, jnp.exp(cand - best), 0.0), axis=0, keepdims=True)
        pick_ref[...] = jnp.where(chosen, 1.0, 0.0)
        n1 = jnp.zeros_like(s1)
        pos = 0
        for a, width in enumerate(_PAIR_WIDTHS):
            count = jnp.sum(pick_ref[pos:pos + width, :], axis=0, keepdims=True)
            n1 = jnp.where(rank1 == float(a), count, n1)
            pos += width
        r2_ref[h] = rank2.astype(BF16)
        e2_ref[h] = jnp.exp(s2 - top2[0]).astype(BF16)
        n1_ref[h] = n1
        e1_ref[h] = jnp.exp(s1 - top1[0]) / z

    def head_stable(h):
        s1, s2 = scores(h)
        (rank1, top1, _), (rank2, top2, _) = _topk_ranks([s1, s2], True)
        cand = pair_sums(top1, top2)
        ((crank, _, _),) = _topk_ranks([cand], True)
        finish(h, s1, s2, rank1, top1, rank2, top2, cand, crank)

    pending = None
    for h in range(P_HEADS + 1):
        xs = []
        if h < P_HEADS:
            s1, s2 = scores(h)
            xs += [s1, s2]
        if pending is not None:
            ph, ps1, ps2, (prank1, ptop1, pn1), (prank2, ptop2, pn2) = pending
            cand = pair_sums(ptop1, ptop2)
            xs.append(cand)
        res = _topk_ranks(xs, False)
        if pending is not None:
            crank, _, cn = res[-1]
            finish(ph, ps1, ps2, prank1, ptop1, prank2, ptop2, cand, crank)
            ties = jnp.abs(pn1 - k) + jnp.abs(pn2 - k) + jnp.abs(cn - k)

            @pl.when(jnp.max(ties) > 0.0)
            def _():
                head_stable(ph)
        if h < P_HEADS:
            pending = (h, s1, s2, res[0], res[1])


def _peer_select(qp, sub_keys_b, tm):
    t = qp.shape[0]
    blk = pl.BlockSpec((P_HEADS, N_KEYS, tm), lambda i: (0, 0, i))
    out = lambda dt: jax.ShapeDtypeStruct((P_HEADS, N_KEYS, t), dt)
    return pl.pallas_call(
        _peer_select_kernel,
        grid=(t // tm,),
        in_specs=[pl.BlockSpec((tm, D_MODEL), lambda i: (i, 0)),
                  _resident(sub_keys_b.shape)],
        out_specs=[blk] * 4,
        out_shape=[out(BF16), out(BF16), out(F32), out(F32)],
        scratch_shapes=[pltpu.VMEM((P_TOPK, tm), F32),
                        pltpu.VMEM((PAIR_ROWS, tm), F32),
                        pltpu.VMEM((PAIR_ROWS, tm), F32)],
        compiler_params=_params("arbitrary"),
        name="peer_select",
    )(qp, sub_keys_b)


def _gelu(h):
    return 0.5 * h * (1.0 + lax.erf(h * (2.0 ** -0.5)))


BF16_ROWS = 2 * SUBLANES


DENSE_BLOCK = 2 * N_KEYS


def _peer_dense_kernel(xnt_ref, x1_ref, u_ref, v_ref, r2_ref, e2_ref, n1_ref, e1_ref,
                       o_ref, *, te):
    j = pl.program_id(1)
    tm = xnt_ref.shape[1]
    zero = jnp.zeros((), BF16)

    @pl.when(j == 0)
    def _():
        o_ref[...] = x1_ref[...]

    def token_row(ref, h, i1):
        row = jnp.broadcast_to(ref[h, pl.ds(i1, 1), :], (BF16_ROWS, tm))
        return row.astype(BF16)[None]

    def hidden(b, part):
        rows = slice(b * DENSE_BLOCK, (b + 1) * DENSE_BLOCK)
        cols = slice(part * (tm // n_parts), (part + 1) * (tm // n_parts))
        return jnp.dot(u_ref[rows, :], xnt_ref[:, cols], preferred_element_type=F32)

    n_blocks = te // DENSE_BLOCK
    n_parts = DENSE_BLOCK // N_KEYS
    ht_next = [hidden(0, part) for part in range(n_parts)]
    for b in range(n_blocks):
        rows = slice(b * DENSE_BLOCK, (b + 1) * DENSE_BLOCK)
        ht = jnp.concatenate(ht_next, axis=-1)
        ht_next = []
        parts = []
        for kb in range(n_parts):
            if b + 1 < n_blocks:
                ht_next.append(hidden(b + 1, kb))
            i1 = j * (te // N_KEYS) + b * n_parts + kb
            gate = None
            for h in range(P_HEADS):
                term = token_row(e1_ref, h, i1) * jnp.where(
                    r2_ref[h] < token_row(n1_ref, h, i1), e2_ref[h], zero)
                gate = term if gate is None else gate + term
            act = _gelu(ht[kb * N_KEYS:(kb + 1) * N_KEYS].astype(BF16)).reshape(gate.shape)
            parts.append((act * gate).reshape(N_KEYS, tm))
        at = jnp.concatenate(parts, axis=0)
        o_ref[...] += lax.dot_general(at, v_ref[rows, :], _TN, preferred_element_type=F32)


def _peer_dense(xnt, x1, eu_b, ev_b, r2, e2, n1, e1, tm, te):
    t = x1.shape[0]
    n_exp = eu_b.shape[0]
    once = pl.Buffered(1)
    tok = lambda i, j: (i, 0)
    exp = pl.BlockSpec((te, D_MODEL), lambda i, j: (j, 0))
    packed = pl.BlockSpec((P_HEADS, N_KEYS // BF16_ROWS, BF16_ROWS, tm),
                          lambda i, j: (0, 0, 0, i), pipeline_mode=once)
    rows = pl.BlockSpec((P_HEADS, N_KEYS, tm), lambda i, j: (0, 0, i), pipeline_mode=once)
    split = lambda y: y.reshape(P_HEADS, N_KEYS // BF16_ROWS, BF16_ROWS, t)
    return pl.pallas_call(
        functools.partial(_peer_dense_kernel, te=te),
        grid=(t // tm, n_exp // te),
        in_specs=[pl.BlockSpec((D_MODEL, tm), lambda i, j: (0, i), pipeline_mode=once),
                  pl.BlockSpec((tm, D_MODEL), tok, pipeline_mode=once),
                  exp, exp, packed, packed, rows, rows],
        out_specs=pl.BlockSpec((tm, D_MODEL), tok),
        out_shape=jax.ShapeDtypeStruct((t, D_MODEL), F32),
        compiler_params=_params("arbitrary", "arbitrary"),
        name="peer_dense",
    )(xnt, x1, eu_b, ev_b, split(r2), split(e2), n1, e1)


def _tiles():
    return dict(proj_rows=256, attn_q=512, conv_rows=256, select_tokens=256,
                dense_tokens=512, dense_experts=1024, sample_pages=8, sample_rows=128)


def _block_ones(width, group):
    r = jnp.arange(width) // group
    return (r[:, None] == r[None, :]).astype(BF16)


def _finish(x2d, a, c, w_out_b, norm_ffn_g, w_query_b, sub_keys_b, eu_b, ev_b, tiles, tm):
    x1, xnt, qp = _out_proj(x2d, a, c, w_out_b, norm_ffn_g, w_query_b, tm)
    sel = _peer_select(qp, sub_keys_b, min(tiles["select_tokens"], x2d.shape[0]))
    return _peer_dense(xnt, x1, eu_b, ev_b, *sel,
                       min(tiles["dense_tokens"], x2d.shape[0]), tiles["dense_experts"])


def kernel(x_prompt, x_sample, cache_k, cache_v, state_conv, page_table, norm_mix_g, w_in,
           q_norm_g, k_norm_g, lambda_q1, lambda_k1, lambda_q2, lambda_k2, subln_g, w_dw, b_dw,
           conv_norm_g, w_out, norm_ffn_g, w_query, sub_keys, expert_u, expert_v):
    tiles = _tiles()
    b, s, _ = x_prompt.shape
    bd = x_sample.shape[0]
    assert x_sample.shape[1] == 1

    w_in_b = w_in.astype(BF16)
    w_out_b = w_out.astype(BF16)
    w_query_b = w_query.astype(BF16)
    sub_keys_b = sub_keys.astype(BF16)
    eu_b = expert_u.astype(BF16)
    ev_b = expert_v.astype(BF16)
    row = lambda v: v.reshape(1, -1)
    qg = row(jnp.tile(q_norm_g, ATTN_W // HEAD_DIM))
    kg = row(jnp.tile(k_norm_g, ATTN_W // HEAD_DIM))
    lam4 = jnp.stack([lambda_q1, lambda_k1, lambda_q2, lambda_k2])
    pmat = _block_ones(2 * LANES, HEAD_DIM)
    sg, bdw, cg = row(subln_g), row(b_dw), row(conv_norm_g)
    nmix, nffn = row(norm_mix_g), row(norm_ffn_g)

    xp = x_prompt.reshape(b * s, D_MODEL)
    qb, k, kb, v, vb, u = _in_proj(xp, nmix, w_in_b, qg, kg, pmat, tiles["proj_rows"])
    r3 = lambda y: y.reshape(b, s, -1)
    a_p = _attn_prompt(lam4, sg, r3(qb), r3(kb), r3(vb), tiles["attn_q"])
    c_p = _conv_prompt(r3(u), w_dw, bdw, cg, tiles["conv_rows"])
    y_p = _finish(xp, a_p.reshape(b * s, ATTN_W), c_p.reshape(b * s, CONV_W), w_out_b, nffn,
                  w_query_b, sub_keys_b, eu_b, ev_b, tiles, tiles["proj_rows"])

    rows = tiles["sample_rows"]
    xs = jnp.pad(x_sample.reshape(bd, D_MODEL), ((0, rows - bd), (0, 0)))
    qs, ks, _, vs, _, us = _in_proj(xs, nmix, w_in_b, qg, kg, pmat, rows)
    ks, vs, us = ks[:bd], vs[:bd], us[:bd]
    h3 = lambda y: y.reshape(bd, N_HEADS, HEAD_W)
    a_s = _attn_sample(page_table, lam4, sg, h3(qs[:bd].astype(F32)), h3(ks), h3(vs),
                       cache_k, cache_v, tiles["sample_pages"])
    c_s = _conv_sample(state_conv, us, w_dw, bdw, cg)
    pad = lambda y: jnp.pad(y, ((0, rows - bd), (0, 0)))
    y_s = _finish(xs, pad(a_s.reshape(bd, ATTN_W)), pad(c_s), w_out_b, nffn,
                  w_query_b, sub_keys_b, eu_b, ev_b, tiles, rows)[:bd]

    heads = lambda y, n: y.reshape(n, -1, N_HEADS, HEAD_W)
    conv_prompt = r3(u)[:, s - CONV_STATE:]
    conv_sample = jnp.concatenate([state_conv[:, 1:], us[:, None, :]], axis=1)
    return (y_p.reshape(b, s, D_MODEL), y_s.reshape(bd, 1, D_MODEL),
            heads(k, b), heads(v, b), conv_prompt,
            heads(ks, bd), heads(vs, bd), conv_sample)
```

```python
import functools
import math

import jax
import jax.numpy as jnp
from jax import lax
from jax.experimental import pallas as pl
from jax.experimental.pallas import tpu as pltpu

F32 = jnp.float32
BF16 = jnp.bfloat16

D_MODEL = 2048
N_HEADS = 8
HEAD_DIM = 64
HEAD_W = 2 * HEAD_DIM
ATTN_W = N_HEADS * HEAD_W
CONV_W = D_MODEL - ATTN_W
CONV_WIDTH = 31
CONV_STATE = CONV_WIDTH - 1
LAMBDA_INIT = 0.8 - 0.6 * math.exp(-0.3 * (1 - 1))
SCALE = HEAD_DIM ** -0.5
PAGE_SIZE = 128
N_KEYS = 128
P_HEADS = 8
P_TOPK = 16
EPS = 1e-6
NEG = -1e30
LOG2E = math.log2(math.e)

SUBLANES = 8
LANES = 128
VMEM_LIMIT_BYTES = 56 * 1024 * 1024

_NT = (((1,), (1,)), ((), ()))
_TN = (((0,), (0,)), ((), ()))


def _params(*sem):
    return pltpu.CompilerParams(dimension_semantics=sem, vmem_limit_bytes=VMEM_LIMIT_BYTES)


def _resident(shape):
    return pl.BlockSpec(shape, lambda *_: (0,) * len(shape), pipeline_mode=pl.Buffered(1))


def _rms(x, gain):
    ms = jnp.mean(x * x, axis=-1, keepdims=True)
    return x * lax.rsqrt(ms + EPS) * gain


def _sigmoid(x):
    return 1.0 / (1.0 + jnp.exp(-x))


def _diff_lambda(lam_ref):
    l = lam_ref[...]
    e1 = jnp.exp(jnp.sum(l[0:1] * l[1:2], axis=-1, keepdims=True))
    e2 = jnp.exp(jnp.sum(l[2:3] * l[3:4], axis=-1, keepdims=True))
    return e1 - e2 + LAMBDA_INIT


def _in_proj_kernel(x_ref, g_ref, w_ref, qg_ref, kg_ref, p_ref,
                    qb_ref, k_ref, kb_ref, v_ref, vb_ref, u_ref):
    xn = _rms(x_ref[...], g_ref[...]).astype(BF16)

    def proj(col):
        return jnp.dot(xn, w_ref[:, col:col + ATTN_W], preferred_element_type=F32)

    def head_norm(y, gain):
        y2 = y * y
        hi = y2.astype(BF16)
        lo = (y2 - hi.astype(F32)).astype(BF16)
        p = p_ref[...]
        w = p.shape[0]
        parts = []
        for c in range(ATTN_W // w):
            sl = slice(c * w, (c + 1) * w)
            parts.append(jnp.dot(hi[:, sl], p, preferred_element_type=F32)
                         + jnp.dot(lo[:, sl], p, preferred_element_type=F32))
        ss = jnp.concatenate(parts, axis=-1)
        return y * lax.rsqrt(ss * (1.0 / HEAD_DIM) + EPS) * gain

    q = head_norm(proj(0), qg_ref[...])
    qb_ref[...] = (q * (SCALE * LOG2E)).astype(BF16)
    k = head_norm(proj(ATTN_W), kg_ref[...])
    k_ref[...] = k
    kb_ref[...] = k.astype(BF16)
    v = proj(2 * ATTN_W)
    v_ref[...] = v
    vb_ref[...] = v.astype(BF16)
    a = proj(3 * ATTN_W)
    gt = proj(3 * ATTN_W + CONV_W)
    u_ref[...] = a * _sigmoid(gt)


def _in_proj(x2d, norm_g, w_in_b, qg, kg, pmat, tm):
    t = x2d.shape[0]
    row = lambda i: (i, 0)
    blk = lambda: pl.BlockSpec((tm, ATTN_W), row)
    return pl.pallas_call(
        _in_proj_kernel,
        grid=(t // tm,),
        in_specs=[pl.BlockSpec((tm, D_MODEL), row),
                  _resident((1, D_MODEL)),
                  _resident(w_in_b.shape),
                  _resident((1, ATTN_W)),
                  _resident((1, ATTN_W)),
                  _resident(pmat.shape)],
        out_specs=[blk(), blk(), blk(), blk(), blk(), blk()],
        out_shape=[jax.ShapeDtypeStruct((t, ATTN_W), BF16),
                   jax.ShapeDtypeStruct((t, ATTN_W), F32),
                   jax.ShapeDtypeStruct((t, ATTN_W), BF16),
                   jax.ShapeDtypeStruct((t, ATTN_W), F32),
                   jax.ShapeDtypeStruct((t, ATTN_W), BF16),
                   jax.ShapeDtypeStruct((t, CONV_W), F32)],
        compiler_params=_params("arbitrary"),
        name="in_proj",
    )(x2d, norm_g, w_in_b, qg, kg, pmat)


def _lane_fold(x, op):
    out = x[:, :LANES]
    for c in range(1, x.shape[1] // LANES):
        out = op(out, x[:, c * LANES:(c + 1) * LANES])
    return out


def _attn_prompt_kernel(lam_ref, sg_ref, q_ref, k_ref, v_ref, o_ref, s1_ref, s2_ref, *, tq):
    lam = _diff_lambda(lam_ref)
    seq = q_ref.shape[1]
    lane = lax.broadcasted_iota(jnp.int32, (tq, HEAD_W), 1)
    r = lax.broadcasted_iota(jnp.int32, (tq, tq), 0)
    c = lax.broadcasted_iota(jnp.int32, (tq, tq), 1)
    keep = c <= r
    for i in range(seq // tq):
        rows = slice(i * tq, (i + 1) * tq)
        q = q_ref[0, rows, :]
        zero = jnp.zeros_like(q)
        qs = (jnp.where(lane < HEAD_DIM, q, zero), jnp.where(lane >= HEAD_DIM, q, zero))
        tops = [jnp.full((tq, LANES), NEG, F32) for _ in qs]
        for j in range(i + 1):
            cols = slice(j * tq, (j + 1) * tq)
            kj = k_ref[0, cols, :]
            for n, s_ref in enumerate((s1_ref, s2_ref)):
                s = lax.dot_general(qs[n], kj, _NT, preferred_element_type=F32)
                if j == i:
                    s = jnp.where(keep, s, NEG)
                s_ref[:, cols] = s
                tops[n] = jnp.maximum(tops[n], _lane_fold(s, jnp.maximum))
        outs = []
        for n, s_ref in enumerate((s1_ref, s2_ref)):
            m = jnp.max(tops[n], axis=-1, keepdims=True)
            part = jnp.zeros((tq, LANES), F32)
            acc = jnp.zeros((tq, HEAD_W), F32)
            for j in range(i + 1):
                cols = slice(j * tq, (j + 1) * tq)
                p = jnp.exp2(s_ref[:, cols] - m)
                part = part + _lane_fold(p, jnp.add)
                acc = acc + jnp.dot(p.astype(BF16), v_ref[0, cols, :],
                                    preferred_element_type=F32)
            outs.append(acc / jnp.sum(part, axis=-1, keepdims=True))
        o = outs[0] - lam * outs[1]
        o_ref[0, rows, :] = (_rms(o, sg_ref[...]) * (1.0 - LAMBDA_INIT)).astype(o_ref.dtype)


def _attn_prompt(lam4, subln_g, qb, kb, vb, tq):
    b, s, _ = qb.shape
    head = pl.BlockSpec((1, s, HEAD_W), lambda bi, h: (bi, 0, h))
    return pl.pallas_call(
        functools.partial(_attn_prompt_kernel, tq=tq),
        grid=(b, N_HEADS),
        in_specs=[_resident(lam4.shape), _resident((1, HEAD_W)), head, head, head],
        out_specs=head,
        out_shape=jax.ShapeDtypeStruct((b, s, ATTN_W), BF16),
        scratch_shapes=[pltpu.VMEM((tq, s), F32), pltpu.VMEM((tq, s), F32)],
        compiler_params=_params("arbitrary", "arbitrary"),
        name="attn_prompt",
    )(lam4, subln_g, qb, kb, vb)


def _attn_sample_kernel(pt_ref, lam_ref, sg_ref, q_ref, kn_ref, vn_ref, *rest, pages):
    k_refs = rest[:pages]
    v_refs = rest[pages:2 * pages]
    o_ref = rest[2 * pages]
    m_ref, l_ref, acc_ref = rest[2 * pages + 1:]
    p = pl.program_id(1)
    q = q_ref[0]
    lane = lax.broadcasted_iota(jnp.int32, q.shape, 1)
    zero = jnp.zeros_like(q)
    q_rows = jnp.concatenate([jnp.where(lane < HEAD_DIM, q, zero),
                              jnp.where(lane >= HEAD_DIM, q, zero)], axis=0)
    cols = PAGE_SIZE * N_HEADS
    row_head = lax.broadcasted_iota(jnp.int32, (2 * N_HEADS, cols), 0) % N_HEADS
    col_head = lax.broadcasted_iota(jnp.int32, (2 * N_HEADS, cols), 1) % N_HEADS
    own_head = row_head == col_head

    @pl.when(p == 0)
    def _():
        kn = kn_ref[0]
        vn = vn_ref[0]
        s_new = jnp.sum(q_rows * jnp.concatenate([kn, kn], axis=0), axis=-1, keepdims=True)
        m_ref[...] = jnp.broadcast_to(s_new, m_ref.shape)
        l_ref[...] = jnp.ones_like(l_ref)
        acc_ref[...] = jnp.concatenate([vn, vn], axis=0)

    m = m_ref[:, 0:1]
    l = l_ref[:, 0:1]
    qb = q_rows.astype(BF16)
    scores = []
    for r in range(pages):
        kb = k_refs[r][0].reshape(cols, HEAD_W).astype(BF16)
        s = lax.dot_general(qb, kb, _NT, preferred_element_type=F32)
        scores.append(jnp.where(own_head, s, NEG))
    top = scores[0]
    for s in scores[1:]:
        top = jnp.maximum(top, s)
    m_new = jnp.maximum(m, jnp.max(top, axis=-1, keepdims=True))
    corr = jnp.exp2(m - m_new)
    l = l * corr
    acc = acc_ref[...] * corr
    for r in range(pages):
        pr = jnp.exp2(scores[r] - m_new)
        l = l + jnp.sum(pr, axis=-1, keepdims=True)
        vb = v_refs[r][0].reshape(cols, HEAD_W).astype(BF16)
        acc = acc + jnp.dot(pr.astype(BF16), vb, preferred_element_type=F32)
    m_ref[...] = jnp.broadcast_to(m_new, m_ref.shape)
    l_ref[...] = jnp.broadcast_to(l, l_ref.shape)
    acc_ref[...] = acc

    @pl.when(p == pl.num_programs(1) - 1)
    def _():
        o = acc_ref[...] / l_ref[...]
        w = o[:N_HEADS] - _diff_lambda(lam_ref) * o[N_HEADS:]
        o_ref[0] = (_rms(w, sg_ref[...]) * (1.0 - LAMBDA_INIT)).astype(o_ref.dtype)


def _attn_sample(page_table, lam4, subln_g, q, k_new, v_new, cache_k, cache_v, pages):
    bd, n_pages = page_table.shape
    page_blk = (1, PAGE_SIZE, N_HEADS, HEAD_W)

    def page_spec(r):
        return pl.BlockSpec(page_blk, lambda b, p, pt: (pt[b, p * pages + r], 0, 0, 0))

    tok = pl.BlockSpec((1, N_HEADS, HEAD_W), lambda b, p, pt: (b, 0, 0))
    const = lambda shape: pl.BlockSpec(shape, lambda b, p, pt: (0,) * len(shape))
    state = pltpu.VMEM((2 * N_HEADS, HEAD_W), F32)
    grid_spec = pltpu.PrefetchScalarGridSpec(
        num_scalar_prefetch=1,
        grid=(bd, n_pages // pages),
        in_specs=[const(lam4.shape), const((1, HEAD_W)), tok, tok, tok]
                 + [page_spec(r) for r in range(pages)]
                 + [page_spec(r) for r in range(pages)],
        out_specs=tok,
        scratch_shapes=[state, state, state],
    )
    return pl.pallas_call(
        functools.partial(_attn_sample_kernel, pages=pages),
        grid_spec=grid_spec,
        out_shape=jax.ShapeDtypeStruct((bd, N_HEADS, HEAD_W), BF16),
        compiler_params=_params("arbitrary", "arbitrary"),
        name="attn_sample",
    )(page_table, lam4, subln_g, q, k_new, v_new,
      *([cache_k] * pages), *([cache_v] * pages))


CONV_HALO = 32
CONV_ROWS = 64


def _conv_finish(y, g_ref):
    c = _rms(y, g_ref[...])
    return (c * _sigmoid(c)).astype(BF16)


def _conv_prompt_kernel(u_ref, halo_ref, w_ref, b_ref, g_ref, o_ref, win_ref, sh_ref, y_ref,
                        *, ts):
    i = pl.program_id(1)
    halo = halo_ref[0]
    win_ref[0:CONV_HALO, :] = jnp.where(i == 0, jnp.zeros_like(halo), halo)
    win_ref[CONV_HALO:, :] = u_ref[0]
    shift = CONV_HALO - CONV_STATE
    span = sh_ref.shape[1]

    def lane_chunk(c, _):
        lanes = pl.ds(pl.multiple_of(c * LANES, LANES), LANES)
        for phase in range(1, SUBLANES):
            sh_ref[phase] = win_ref[phase:phase + span, lanes]
        bias = b_ref[:, lanes]
        for r in range(ts // CONV_ROWS):
            acc = jnp.zeros((CONV_ROWS, LANES), F32) + bias
            for j in range(CONV_WIDTH):
                phase = (j + shift) % SUBLANES
                row0 = r * CONV_ROWS + j + shift - phase
                if phase == 0:
                    rows = win_ref[row0:row0 + CONV_ROWS, lanes]
                else:
                    rows = sh_ref[phase, row0:row0 + CONV_ROWS, :]
                acc = acc + w_ref[j:j + 1, lanes] * rows
            y_ref[r * CONV_ROWS:(r + 1) * CONV_ROWS, lanes] = acc
        return 0

    lax.fori_loop(0, CONV_W // LANES, lane_chunk, 0)
    o_ref[0] = _conv_finish(y_ref[...], g_ref)


def _conv_prompt(u, w_dw, b_dw, g, ts):
    b, s, _ = u.shape
    per = ts // CONV_HALO
    return pl.pallas_call(
        functools.partial(_conv_prompt_kernel, ts=ts),
        grid=(b, s // ts),
        in_specs=[pl.BlockSpec((1, ts, CONV_W), lambda bi, i: (bi, i, 0)),
                  pl.BlockSpec((1, CONV_HALO, CONV_W),
                               lambda bi, i: (bi, jnp.maximum(i * per - 1, 0), 0)),
                  _resident(w_dw.shape),
                  _resident((1, CONV_W)),
                  _resident((1, CONV_W))],
        out_specs=pl.BlockSpec((1, ts, CONV_W), lambda bi, i: (bi, i, 0)),
        out_shape=jax.ShapeDtypeStruct((b, s, CONV_W), BF16),
        scratch_shapes=[pltpu.VMEM((CONV_HALO + ts, CONV_W), F32),
                        pltpu.VMEM((SUBLANES, CONV_HALO + ts - SUBLANES, LANES), F32),
                        pltpu.VMEM((ts, CONV_W), F32)],
        compiler_params=_params("arbitrary", "arbitrary"),
        name="conv_prompt",
    )(u, u, w_dw, b_dw, g)


def _conv_sample_kernel(st_ref, u_ref, w_ref, b_ref, g_ref, o_ref):
    acc = b_ref[...] + w_ref[CONV_STATE:CONV_WIDTH, :] * u_ref[...]
    for j in range(CONV_STATE):
        acc = acc + w_ref[j:j + 1, :] * st_ref[:, j, :]
    o_ref[...] = _conv_finish(acc, g_ref)


def _conv_sample(state, u, w_dw, b_dw, g):
    bd = u.shape[0]
    return pl.pallas_call(
        _conv_sample_kernel,
        out_shape=jax.ShapeDtypeStruct((bd, CONV_W), BF16),
        compiler_params=pltpu.CompilerParams(vmem_limit_bytes=VMEM_LIMIT_BYTES),
        name="conv_sample",
    )(state, u, w_dw, b_dw, g)


def _out_proj_kernel(x_ref, a_ref, c_ref, wo_ref, g_ref, wq_ref, x1_ref, xnt_ref, qp_ref):
    x1 = (x_ref[...]
          + jnp.dot(a_ref[...], wo_ref[0:ATTN_W, :], preferred_element_type=F32)
          + jnp.dot(c_ref[...], wo_ref[ATTN_W:D_MODEL, :], preferred_element_type=F32))
    x1_ref[...] = x1
    xn = _rms(x1, g_ref[...])
    xnt_ref[...] = xn.T.astype(BF16)
    qp_ref[...] = jnp.dot(xn.astype(BF16), wq_ref[...],
                          preferred_element_type=F32).astype(BF16)


def _out_proj(x2d, a, c, w_out_b, g, w_query_b, tm):
    t = x2d.shape[0]
    row = lambda i: (i, 0)
    return pl.pallas_call(
        _out_proj_kernel,
        grid=(t // tm,),
        in_specs=[pl.BlockSpec((tm, D_MODEL), row),
                  pl.BlockSpec((tm, ATTN_W), row),
                  pl.BlockSpec((tm, CONV_W), row),
                  _resident(w_out_b.shape),
                  _resident((1, D_MODEL)),
                  _resident(w_query_b.shape)],
        out_specs=[pl.BlockSpec((tm, D_MODEL), row),
                   pl.BlockSpec((D_MODEL, tm), lambda i: (0, i)),
                   pl.BlockSpec((tm, D_MODEL), row)],
        out_shape=[jax.ShapeDtypeStruct((t, D_MODEL), F32),
                   jax.ShapeDtypeStruct((D_MODEL, t), BF16),
                   jax.ShapeDtypeStruct((t, D_MODEL), BF16)],
        compiler_params=_params("arbitrary"),
        name="out_proj",
    )(x2d, a, c, w_out_b, g, w_query_b)


_PAIR_WIDTHS = [P_TOPK // (a + 1) for a in range(P_TOPK)]
N_PAIRS = sum(_PAIR_WIDTHS)
PAIR_ROWS = -(-N_PAIRS // SUBLANES) * SUBLANES


def _topk_rounds(xs, stable):
    xs = list(xs)
    vals = [[] for _ in xs]
    ranks = [None] * len(xs)
    if stable:
        ranks = [jnp.full(x.shape, float(P_TOPK), F32) for x in xs]
        rows = [lax.broadcasted_iota(jnp.int32, x.shape, 0).astype(F32) for x in xs]
    for r in range(P_TOPK):
        for a, x in enumerate(xs):
            m = jnp.max(x, axis=0, keepdims=True)
            if stable:
                first = jnp.min(jnp.where(x == m, rows[a], float(x.shape[0])),
                                axis=0, keepdims=True)
                hit = rows[a] == first
                ranks[a] = jnp.where(hit, float(r), ranks[a])
            else:
                hit = x == m
            xs[a] = jnp.where(hit, -jnp.inf, x)
            vals[a].append(m)
    return list(zip(vals, xs, ranks))


def _count(mask):
    return jnp.sum(jnp.where(mask, 1.0, 0.0), axis=0, keepdims=True)


def _peer_select_kernel(qp_ref, sk_ref, r2_ref, e2_ref, n1_ref, e1_ref,
                        t2_ref, cand_ref, pick_ref):
    tm = qp_ref.shape[0]
    k = float(P_TOPK)
    cand_ref[N_PAIRS:, :] = jnp.full((PAIR_ROWS - N_PAIRS, tm), -jnp.inf, F32)

    def scores(h):
        def score(c):
            q = qp_ref[:, (2 * h + c) * N_KEYS:(2 * h + c + 1) * N_KEYS]
            return lax.dot_general(sk_ref[h, c], q, _NT, preferred_element_type=F32)
        return score(0), score(1)

    def pair_sums(top1, top2):
        for r in range(P_TOPK):
            t2_ref[r:r + 1, :] = top2[r]
        pos = 0
        for a, width in enumerate(_PAIR_WIDTHS):
            cand_ref[pos:pos + width, :] = top1[a] + t2_ref[0:width, :]
            pos += width
        return cand_ref[...]

    def finish(h, s1, s2, top1, top2, cand, chosen, first_list_row, rank2):
        best = top1[0] + top2[0]
        z = jnp.sum(jnp.where(chosen, jnp.exp(cand - best), 0.0), axis=0, keepdims=True)
        pick_ref[...] = jnp.where(chosen, 1.0, 0.0)
        n1 = jnp.zeros_like(s1)
        pos = 0
        for a, width in enumerate(_PAIR_WIDTHS):
            count = jnp.sum(pick_ref[pos:pos + width, :], axis=0, keepdims=True)
            n1 = jnp.where(first_list_row(a), count, n1)
            pos += width
        r2_ref[h] = rank2.astype(BF16)
        e2_ref[h] = jnp.exp(s2 - top2[0]).astype(BF16)
        n1_ref[h] = n1
        e1_ref[h] = jnp.exp(s1 - top1[0]) / z

    def head_stable(h):
        s1, s2 = scores(h)
        (top1, _, rank1), (top2, _, rank2) = _topk_rounds([s1, s2], True)
        cand = pair_sums(top1, top2)
        ((_, _, crank),) = _topk_rounds([cand], True)
        finish(h, s1, s2, top1, top2, cand, crank < k, lambda a: rank1 == float(a), rank2)

    pending = None
    for h in range(P_HEADS + 1):
        xs = []
        if h < P_HEADS:
            s1, s2 = scores(h)
            xs += [s1, s2]
        if pending is not None:
            ph, ps1, ps2, (ptop1, pleft1, _), (ptop2, _, _) = pending
            cand = pair_sums(ptop1, ptop2)
            xs.append(cand)
        res = _topk_rounds(xs, False)
        if pending is not None:
            ctop = res[-1][0]
            chosen = cand >= ctop[P_TOPK - 1]
            rank2 = jnp.zeros_like(ps2)
            for r in range(P_TOPK):
                rank2 = rank2 + jnp.where(ps2 < ptop2[r], 1.0, 0.0)
            finish(ph, ps1, ps2, ptop1, ptop2, cand, chosen, lambda a: ps1 == ptop1[a], rank2)
            ties = (jnp.abs(_count(pleft1 == -jnp.inf) - k) + jnp.abs(_count(rank2 < k) - k)
                    + jnp.abs(_count(chosen) - k))

            @pl.when(jnp.max(ties) > 0.0)
            def _():
                head_stable(ph)
        if h < P_HEADS:
            pending = (h, s1, s2, res[0], res[1])


def _peer_select(qp, sub_keys_b, tm):
    t = qp.shape[0]
    blk = pl.BlockSpec((P_HEADS, N_KEYS, tm), lambda i: (0, 0, i))
    out = lambda dt: jax.ShapeDtypeStruct((P_HEADS, N_KEYS, t), dt)
    return pl.pallas_call(
        _peer_select_kernel,
        grid=(t // tm,),
        in_specs=[pl.BlockSpec((tm, D_MODEL), lambda i: (i, 0)),
                  _resident(sub_keys_b.shape)],
        out_specs=[blk] * 4,
        out_shape=[out(BF16), out(BF16), out(F32), out(F32)],
        scratch_shapes=[pltpu.VMEM((P_TOPK, tm), F32),
                        pltpu.VMEM((PAIR_ROWS, tm), F32),
                        pltpu.VMEM((PAIR_ROWS, tm), F32)],
        compiler_params=_params("arbitrary"),
        name="peer_select",
    )(qp, sub_keys_b)


def _gelu(h):
    return 0.5 * h * (1.0 + lax.erf(h * (2.0 ** -0.5)))


BF16_ROWS = 2 * SUBLANES


DENSE_BLOCK = 2 * N_KEYS


def _peer_dense_kernel(xnt_ref, x1_ref, u_ref, v_ref, r2_ref, e2_ref, n1_ref, e1_ref,
                       o_ref, *, te):
    j = pl.program_id(1)
    tm = xnt_ref.shape[1]
    zero = jnp.zeros((), BF16)

    @pl.when(j == 0)
    def _():
        o_ref[...] = x1_ref[...]

    def token_row(ref, h, i1):
        row = jnp.broadcast_to(ref[h, pl.ds(i1, 1), :], (BF16_ROWS, tm))
        return row.astype(BF16)[None]

    def hidden(b, part):
        rows = slice(b * DENSE_BLOCK, (b + 1) * DENSE_BLOCK)
        cols = slice(part * (tm // n_parts), (part + 1) * (tm // n_parts))
        return jnp.dot(u_ref[rows, :], xnt_ref[:, cols], preferred_element_type=F32)

    n_blocks = te // DENSE_BLOCK
    n_parts = DENSE_BLOCK // N_KEYS
    ht_next = [hidden(0, part) for part in range(n_parts)]
    parts = []
    for b in range(n_blocks):
        ht = jnp.concatenate(ht_next, axis=-1)
        ht_next = []
        for kb in range(n_parts):
            if b + 1 < n_blocks:
                ht_next.append(hidden(b + 1, kb))
            i1 = j * (te // N_KEYS) + b * n_parts + kb
            gate = None
            for h in range(P_HEADS):
                term = token_row(e1_ref, h, i1) * jnp.where(
                    r2_ref[h] < token_row(n1_ref, h, i1), e2_ref[h], zero)
                gate = term if gate is None else gate + term
            act = _gelu(ht[kb * N_KEYS:(kb + 1) * N_KEYS].astype(BF16)).reshape(gate.shape)
            parts.append((act * gate).reshape(N_KEYS, tm))
    at = jnp.concatenate(parts, axis=0)
    o_ref[...] += lax.dot_general(at, v_ref[...], _TN, preferred_element_type=F32)


def _peer_dense(xnt, x1, eu_b, ev_b, r2, e2, n1, e1, tm, te):
    t = x1.shape[0]
    n_exp = eu_b.shape[0]
    once = pl.Buffered(1)
    tok = lambda i, j: (i, 0)
    exp = pl.BlockSpec((te, D_MODEL), lambda i, j: (j, 0))
    packed = pl.BlockSpec((P_HEADS, N_KEYS // BF16_ROWS, BF16_ROWS, tm),
                          lambda i, j: (0, 0, 0, i), pipeline_mode=once)
    rows = pl.BlockSpec((P_HEADS, N_KEYS, tm), lambda i, j: (0, 0, i), pipeline_mode=once)
    split = lambda y: y.reshape(P_HEADS, N_KEYS // BF16_ROWS, BF16_ROWS, t)
    return pl.pallas_call(
        functools.partial(_peer_dense_kernel, te=te),
        grid=(t // tm, n_exp // te),
        in_specs=[pl.BlockSpec((D_MODEL, tm), lambda i, j: (0, i), pipeline_mode=once),
                  pl.BlockSpec((tm, D_MODEL), tok, pipeline_mode=once),
                  exp, exp, packed, packed, rows, rows],
        out_specs=pl.BlockSpec((tm, D_MODEL), tok),
        out_shape=jax.ShapeDtypeStruct((t, D_MODEL), F32),
        compiler_params=_params("arbitrary", "arbitrary"),
        name="peer_dense",
    )(xnt, x1, eu_b, ev_b, split(r2), split(e2), n1, e1)


def _tiles():
    return dict(proj_rows=256, attn_q=512, conv_rows=256, select_tokens=256,
                dense_tokens=512, dense_experts=1024, sample_pages=16, sample_rows=128)


def _block_ones(width, group):
    r = jnp.arange(width) // group
    return (r[:, None] == r[None, :]).astype(BF16)


def _finish(x2d, a, c, w_out_b, norm_ffn_g, w_query_b, sub_keys_b, eu_b, ev_b, tiles, tm):
    x1, xnt, qp = _out_proj(x2d, a, c, w_out_b, norm_ffn_g, w_query_b, tm)
    sel = _peer_select(qp, sub_keys_b, min(tiles["select_tokens"], x2d.shape[0]))
    return _peer_dense(xnt, x1, eu_b, ev_b, *sel,
                       min(tiles["dense_tokens"], x2d.shape[0]), tiles["dense_experts"])


def kernel(x_prompt, x_sample, cache_k, cache_v, state_conv, page_table, norm_mix_g, w_in,
           q_norm_g, k_norm_g, lambda_q1, lambda_k1, lambda_q2, lambda_k2, subln_g, w_dw, b_dw,
           conv_norm_g, w_out, norm_ffn_g, w_query, sub_keys, expert_u, expert_v):
    tiles = _tiles()
    b, s, _ = x_prompt.shape
    bd = x_sample.shape[0]
    assert x_sample.shape[1] == 1

    w_in_b = w_in.astype(BF16)
    w_out_b = w_out.astype(BF16)
    w_query_b = w_query.astype(BF16)
    sub_keys_b = sub_keys.astype(BF16)
    eu_b = expert_u.astype(BF16)
    ev_b = expert_v.astype(BF16)
    row = lambda v: v.reshape(1, -1)
    qg = row(jnp.tile(q_norm_g, ATTN_W // HEAD_DIM))
    kg = row(jnp.tile(k_norm_g, ATTN_W // HEAD_DIM))
    lam4 = jnp.stack([lambda_q1, lambda_k1, lambda_q2, lambda_k2])
    pmat = _block_ones(2 * LANES, HEAD_DIM)
    sg, bdw, cg = row(subln_g), row(b_dw), row(conv_norm_g)
    nmix, nffn = row(norm_mix_g), row(norm_ffn_g)

    xp = x_prompt.reshape(b * s, D_MODEL)
    qb, k, kb, v, vb, u = _in_proj(xp, nmix, w_in_b, qg, kg, pmat, tiles["proj_rows"])
    r3 = lambda y: y.reshape(b, s, -1)
    a_p = _attn_prompt(lam4, sg, r3(qb), r3(kb), r3(vb), tiles["attn_q"])
    c_p = _conv_prompt(r3(u), w_dw, bdw, cg, tiles["conv_rows"])
    y_p = _finish(xp, a_p.reshape(b * s, ATTN_W), c_p.reshape(b * s, CONV_W), w_out_b, nffn,
                  w_query_b, sub_keys_b, eu_b, ev_b, tiles, tiles["proj_rows"])

    rows = tiles["sample_rows"]
    xs = jnp.pad(x_sample.reshape(bd, D_MODEL), ((0, rows - bd), (0, 0)))
    qs, ks, _, vs, _, us = _in_proj(xs, nmix, w_in_b, qg, kg, pmat, rows)
    ks, vs, us = ks[:bd], vs[:bd], us[:bd]
    h3 = lambda y: y.reshape(bd, N_HEADS, HEAD_W)
    a_s = _attn_sample(page_table, lam4, sg, h3(qs[:bd].astype(F32)), h3(ks), h3(vs),
                       cache_k, cache_v, tiles["sample_pages"])
    c_s = _conv_sample(state_conv, us, w_dw, bdw, cg)
    pad = lambda y: jnp.pad(y, ((0, rows - bd), (0, 0)))
    y_s = _finish(xs, pad(a_s.reshape(bd, ATTN_W)), pad(c_s), w_out_b, nffn,
                  w_query_b, sub_keys_b, eu_b, ev_b, tiles, rows)[:bd]

    heads = lambda y, n: y.reshape(n, -1, N_HEADS, HEAD_W)
    conv_prompt = r3(u)[:, s - CONV_STATE:]
    conv_sample = jnp.concatenate([state_conv[:, 1:], us[:, None, :]], axis=1)
    return (y_p.reshape(b, s, D_MODEL), y_s.reshape(bd, 1, D_MODEL),
            heads(k, b), heads(v, b), conv_prompt,
            heads(ks, bd), heads(vs, bd), conv_sample)
```

```python
import functools
import math

import jax
import jax.numpy as jnp
from jax import lax
from jax.experimental import pallas as pl
from jax.experimental.pallas import tpu as pltpu

F32 = jnp.float32
BF16 = jnp.bfloat16

D_MODEL = 2048
N_HEADS = 8
HEAD_DIM = 64
HEAD_W = 2 * HEAD_DIM
ATTN_W = N_HEADS * HEAD_W
CONV_W = D_MODEL - ATTN_W
CONV_WIDTH = 31
CONV_STATE = CONV_WIDTH - 1
LAMBDA_INIT = 0.8 - 0.6 * math.exp(-0.3 * (1 - 1))
SCALE = HEAD_DIM ** -0.5
PAGE_SIZE = 128
N_KEYS = 128
P_HEADS = 8
P_TOPK = 16
EPS = 1e-6
NEG = -1e30
LOG2E = math.log2(math.e)

SUBLANES = 8
LANES = 128
VMEM_LIMIT_BYTES = 56 * 1024 * 1024

_NT = (((1,), (1,)), ((), ()))
_TN = (((0,), (0,)), ((), ()))


def _params(*sem):
    return pltpu.CompilerParams(dimension_semantics=sem, vmem_limit_bytes=VMEM_LIMIT_BYTES)


def _resident(shape):
    return pl.BlockSpec(shape, lambda *_: (0,) * len(shape), pipeline_mode=pl.Buffered(1))


def _rms(x, gain):
    ms = jnp.mean(x * x, axis=-1, keepdims=True)
    return x * lax.rsqrt(ms + EPS) * gain


def _sigmoid(x):
    return 1.0 / (1.0 + jnp.exp(-x))


def _diff_lambda(lam_ref):
    l = lam_ref[...]
    e1 = jnp.exp(jnp.sum(l[0:1] * l[1:2], axis=-1, keepdims=True))
    e2 = jnp.exp(jnp.sum(l[2:3] * l[3:4], axis=-1, keepdims=True))
    return e1 - e2 + LAMBDA_INIT


def _in_proj_kernel(x_ref, g_ref, w_ref, qg_ref, kg_ref, p_ref,
                    qb_ref, k_ref, kb_ref, v_ref, vb_ref, u_ref):
    xn = _rms(x_ref[...], g_ref[...]).astype(BF16)

    def proj(col):
        return jnp.dot(xn, w_ref[:, col:col + ATTN_W], preferred_element_type=F32)

    def head_norm(y, gain):
        y2 = y * y
        hi = y2.astype(BF16)
        lo = (y2 - hi.astype(F32)).astype(BF16)
        p = p_ref[...]
        w = p.shape[0]
        parts = []
        for c in range(ATTN_W // w):
            sl = slice(c * w, (c + 1) * w)
            parts.append(jnp.dot(hi[:, sl], p, preferred_element_type=F32)
                         + jnp.dot(lo[:, sl], p, preferred_element_type=F32))
        ss = jnp.concatenate(parts, axis=-1)
        return y * lax.rsqrt(ss * (1.0 / HEAD_DIM) + EPS) * gain

    q = head_norm(proj(0), qg_ref[...])
    qb_ref[...] = (q * (SCALE * LOG2E)).astype(BF16)
    k = head_norm(proj(ATTN_W), kg_ref[...])
    k_ref[...] = k
    kb_ref[...] = k.astype(BF16)
    v = proj(2 * ATTN_W)
    v_ref[...] = v
    vb_ref[...] = v.astype(BF16)
    a = proj(3 * ATTN_W)
    gt = proj(3 * ATTN_W + CONV_W)
    u_ref[...] = a * _sigmoid(gt)


def _in_proj(x2d, norm_g, w_in_b, qg, kg, pmat, tm):
    t = x2d.shape[0]
    row = lambda i: (i, 0)
    blk = lambda: pl.BlockSpec((tm, ATTN_W), row)
    return pl.pallas_call(
        _in_proj_kernel,
        grid=(t // tm,),
        in_specs=[pl.BlockSpec((tm, D_MODEL), row),
                  _resident((1, D_MODEL)),
                  _resident(w_in_b.shape),
                  _resident((1, ATTN_W)),
                  _resident((1, ATTN_W)),
                  _resident(pmat.shape)],
        out_specs=[blk(), blk(), blk(), blk(), blk(), blk()],
        out_shape=[jax.ShapeDtypeStruct((t, ATTN_W), BF16),
                   jax.ShapeDtypeStruct((t, ATTN_W), F32),
                   jax.ShapeDtypeStruct((t, ATTN_W), BF16),
                   jax.ShapeDtypeStruct((t, ATTN_W), F32),
                   jax.ShapeDtypeStruct((t, ATTN_W), BF16),
                   jax.ShapeDtypeStruct((t, CONV_W), F32)],
        compiler_params=_params("arbitrary"),
        name="in_proj",
    )(x2d, norm_g, w_in_b, qg, kg, pmat)


def _lane_fold(x, op):
    out = x[:, :LANES]
    for c in range(1, x.shape[1] // LANES):
        out = op(out, x[:, c * LANES:(c + 1) * LANES])
    return out


def _attn_prompt_kernel(lam_ref, sg_ref, q_ref, k_ref, v_ref, o_ref, s1_ref, s2_ref, *, tq):
    lam = _diff_lambda(lam_ref)
    seq = q_ref.shape[1]
    lane = lax.broadcasted_iota(jnp.int32, (tq, HEAD_W), 1)
    r = lax.broadcasted_iota(jnp.int32, (tq, tq), 0)
    c = lax.broadcasted_iota(jnp.int32, (tq, tq), 1)
    keep = c <= r
    for i in range(seq // tq):
        rows = slice(i * tq, (i + 1) * tq)
        q = q_ref[0, rows, :]
        zero = jnp.zeros_like(q)
        qs = (jnp.where(lane < HEAD_DIM, q, zero), jnp.where(lane >= HEAD_DIM, q, zero))
        tops = [jnp.full((tq, LANES), NEG, F32) for _ in qs]
        for j in range(i + 1):
            cols = slice(j * tq, (j + 1) * tq)
            kj = k_ref[0, cols, :]
            for n, s_ref in enumerate((s1_ref, s2_ref)):
                s = lax.dot_general(qs[n], kj, _NT, preferred_element_type=F32)
                if j == i:
                    s = jnp.where(keep, s, NEG)
                s_ref[:, cols] = s
                tops[n] = jnp.maximum(tops[n], _lane_fold(s, jnp.maximum))
        outs = []
        for n, s_ref in enumerate((s1_ref, s2_ref)):
            m = jnp.max(tops[n], axis=-1, keepdims=True)
            part = jnp.zeros((tq, LANES), F32)
            acc = jnp.zeros((tq, HEAD_W), F32)
            for j in range(i + 1):
                cols = slice(j * tq, (j + 1) * tq)
                p = jnp.exp2(s_ref[:, cols] - m)
                part = part + _lane_fold(p, jnp.add)
                acc = acc + jnp.dot(p.astype(BF16), v_ref[0, cols, :],
                                    preferred_element_type=F32)
            outs.append(acc / jnp.sum(part, axis=-1, keepdims=True))
        o = outs[0] - lam * outs[1]
        o_ref[0, rows, :] = (_rms(o, sg_ref[...]) * (1.0 - LAMBDA_INIT)).astype(o_ref.dtype)


def _attn_prompt(lam4, subln_g, qb, kb, vb, tq):
    b, s, _ = qb.shape
    head = pl.BlockSpec((1, s, HEAD_W), lambda bi, h: (bi, 0, h))
    return pl.pallas_call(
        functools.partial(_attn_prompt_kernel, tq=tq),
        grid=(b, N_HEADS),
        in_specs=[_resident(lam4.shape), _resident((1, HEAD_W)), head, head, head],
        out_specs=head,
        out_shape=jax.ShapeDtypeStruct((b, s, ATTN_W), BF16),
        scratch_shapes=[pltpu.VMEM((tq, s), F32), pltpu.VMEM((tq, s), F32)],
        compiler_params=_params("arbitrary", "arbitrary"),
        name="attn_prompt",
    )(lam4, subln_g, qb, kb, vb)


def _attn_sample_kernel(pt_ref, lam_ref, sg_ref, q_ref, kn_ref, vn_ref, *rest, pages):
    k_refs = rest[:pages]
    v_refs = rest[pages:2 * pages]
    o_ref = rest[2 * pages]
    m_ref, l_ref, acc_ref = rest[2 * pages + 1:]
    p = pl.program_id(1)
    q = q_ref[0]
    lane = lax.broadcasted_iota(jnp.int32, q.shape, 1)
    zero = jnp.zeros_like(q)
    q_rows = jnp.concatenate([jnp.where(lane < HEAD_DIM, q, zero),
                              jnp.where(lane >= HEAD_DIM, q, zero)], axis=0)
    cols = PAGE_SIZE * N_HEADS
    row_head = lax.broadcasted_iota(jnp.int32, (2 * N_HEADS, cols), 0) % N_HEADS
    col_head = lax.broadcasted_iota(jnp.int32, (2 * N_HEADS, cols), 1) % N_HEADS
    own_head = row_head == col_head

    @pl.when(p == 0)
    def _():
        kn = kn_ref[0]
        vn = vn_ref[0]
        s_new = jnp.sum(q_rows * jnp.concatenate([kn, kn], axis=0), axis=-1, keepdims=True)
        m_ref[...] = jnp.broadcast_to(s_new, m_ref.shape)
        l_ref[...] = jnp.ones_like(l_ref)
        acc_ref[...] = jnp.concatenate([vn, vn], axis=0)

    m = m_ref[:, 0:1]
    l = l_ref[:, 0:1]
    qb = q_rows.astype(BF16)
    scores = []
    for r in range(pages):
        kb = k_refs[r][0].reshape(cols, HEAD_W).astype(BF16)
        s = lax.dot_general(qb, kb, _NT, preferred_element_type=F32)
        scores.append(jnp.where(own_head, s, NEG))
    top = scores[0]
    for s in scores[1:]:
        top = jnp.maximum(top, s)
    m_new = jnp.maximum(m, jnp.max(top, axis=-1, keepdims=True))
    corr = jnp.exp2(m - m_new)
    l = l * corr
    acc = acc_ref[...] * corr
    for r in range(pages):
        pr = jnp.exp2(scores[r] - m_new)
        l = l + jnp.sum(pr, axis=-1, keepdims=True)
        vb = v_refs[r][0].reshape(cols, HEAD_W).astype(BF16)
        acc = acc + jnp.dot(pr.astype(BF16), vb, preferred_element_type=F32)
    m_ref[...] = jnp.broadcast_to(m_new, m_ref.shape)
    l_ref[...] = jnp.broadcast_to(l, l_ref.shape)
    acc_ref[...] = acc

    @pl.when(p == pl.num_programs(1) - 1)
    def _():
        o = acc_ref[...] / l_ref[...]
        w = o[:N_HEADS] - _diff_lambda(lam_ref) * o[N_HEADS:]
        o_ref[0] = (_rms(w, sg_ref[...]) * (1.0 - LAMBDA_INIT)).astype(o_ref.dtype)


def _attn_sample(page_table, lam4, subln_g, q, k_new, v_new, cache_k, cache_v, pages):
    bd, n_pages = page_table.shape
    page_blk = (1, PAGE_SIZE, N_HEADS, HEAD_W)

    def page_spec(r):
        return pl.BlockSpec(page_blk, lambda b, p, pt: (pt[b, p * pages + r], 0, 0, 0))

    tok = pl.BlockSpec((1, N_HEADS, HEAD_W), lambda b, p, pt: (b, 0, 0))
    const = lambda shape: pl.BlockSpec(shape, lambda b, p, pt: (0,) * len(shape))
    state = pltpu.VMEM((2 * N_HEADS, HEAD_W), F32)
    grid_spec = pltpu.PrefetchScalarGridSpec(
        num_scalar_prefetch=1,
        grid=(bd, n_pages // pages),
        in_specs=[const(lam4.shape), const((1, HEAD_W)), tok, tok, tok]
                 + [page_spec(r) for r in range(pages)]
                 + [page_spec(r) for r in range(pages)],
        out_specs=tok,
        scratch_shapes=[state, state, state],
    )
    return pl.pallas_call(
        functools.partial(_attn_sample_kernel, pages=pages),
        grid_spec=grid_spec,
        out_shape=jax.ShapeDtypeStruct((bd, N_HEADS, HEAD_W), BF16),
        compiler_params=_params("arbitrary", "arbitrary"),
        name="attn_sample",
    )(page_table, lam4, subln_g, q, k_new, v_new,
      *([cache_k] * pages), *([cache_v] * pages))


CONV_HALO = 32
CONV_ROWS = 64


def _conv_finish(y, g_ref):
    c = _rms(y, g_ref[...])
    return (c * _sigmoid(c)).astype(BF16)


def _conv_prompt_kernel(u_ref, halo_ref, w_ref, b_ref, g_ref, o_ref, win_ref, sh_ref, y_ref,
                        *, ts):
    i = pl.program_id(1)
    halo = halo_ref[0]
    win_ref[0:CONV_HALO, :] = jnp.where(i == 0, jnp.zeros_like(halo), halo)
    win_ref[CONV_HALO:, :] = u_ref[0]
    shift = CONV_HALO - CONV_STATE
    span = sh_ref.shape[1]

    def lane_chunk(c, _):
        lanes = pl.ds(pl.multiple_of(c * LANES, LANES), LANES)
        for phase in range(1, SUBLANES):
            sh_ref[phase] = win_ref[phase:phase + span, lanes]
        bias = b_ref[:, lanes]
        for r in range(ts // CONV_ROWS):
            acc = jnp.zeros((CONV_ROWS, LANES), F32) + bias
            for j in range(CONV_WIDTH):
                phase = (j + shift) % SUBLANES
                row0 = r * CONV_ROWS + j + shift - phase
                if phase == 0:
                    rows = win_ref[row0:row0 + CONV_ROWS, lanes]
                else:
                    rows = sh_ref[phase, row0:row0 + CONV_ROWS, :]
                acc = acc + w_ref[j:j + 1, lanes] * rows
            y_ref[r * CONV_ROWS:(r + 1) * CONV_ROWS, lanes] = acc
        return 0

    lax.fori_loop(0, CONV_W // LANES, lane_chunk, 0)
    o_ref[0] = _conv_finish(y_ref[...], g_ref)


def _conv_prompt(u, w_dw, b_dw, g, ts):
    b, s, _ = u.shape
    per = ts // CONV_HALO
    return pl.pallas_call(
        functools.partial(_conv_prompt_kernel, ts=ts),
        grid=(b, s // ts),
        in_specs=[pl.BlockSpec((1, ts, CONV_W), lambda bi, i: (bi, i, 0)),
                  pl.BlockSpec((1, CONV_HALO, CONV_W),
                               lambda bi, i: (bi, jnp.maximum(i * per - 1, 0), 0)),
                  _resident(w_dw.shape),
                  _resident((1, CONV_W)),
                  _resident((1, CONV_W))],
        out_specs=pl.BlockSpec((1, ts, CONV_W), lambda bi, i: (bi, i, 0)),
        out_shape=jax.ShapeDtypeStruct((b, s, CONV_W), BF16),
        scratch_shapes=[pltpu.VMEM((CONV_HALO + ts, CONV_W), F32),
                        pltpu.VMEM((SUBLANES, CONV_HALO + ts - SUBLANES, LANES), F32),
                        pltpu.VMEM((ts, CONV_W), F32)],
        compiler_params=_params("arbitrary", "arbitrary"),
        name="conv_prompt",
    )(u, u, w_dw, b_dw, g)


def _conv_sample_kernel(st_ref, u_ref, w_ref, b_ref, g_ref, o_ref):
    acc = b_ref[...] + w_ref[CONV_STATE:CONV_WIDTH, :] * u_ref[...]
    for j in range(CONV_STATE):
        acc = acc + w_ref[j:j + 1, :] * st_ref[:, j, :]
    o_ref[...] = _conv_finish(acc, g_ref)


def _conv_sample(state, u, w_dw, b_dw, g):
    bd = u.shape[0]
    return pl.pallas_call(
        _conv_sample_kernel,
        out_shape=jax.ShapeDtypeStruct((bd, CONV_W), BF16),
        compiler_params=pltpu.CompilerParams(vmem_limit_bytes=VMEM_LIMIT_BYTES),
        name="conv_sample",
    )(state, u, w_dw, b_dw, g)


def _out_proj_kernel(x_ref, a_ref, c_ref, wo_ref, g_ref, wq_ref, x1_ref, xnt_ref, qp_ref):
    x1 = (x_ref[...]
          + jnp.dot(a_ref[...], wo_ref[0:ATTN_W, :], preferred_element_type=F32)
          + jnp.dot(c_ref[...], wo_ref[ATTN_W:D_MODEL, :], preferred_element_type=F32))
    x1_ref[...] = x1
    xn = _rms(x1, g_ref[...])
    xnt_ref[...] = xn.T.astype(BF16)
    qp_ref[...] = jnp.dot(xn.astype(BF16), wq_ref[...],
                          preferred_element_type=F32).astype(BF16)


def _out_proj(x2d, a, c, w_out_b, g, w_query_b, tm):
    t = x2d.shape[0]
    row = lambda i: (i, 0)
    return pl.pallas_call(
        _out_proj_kernel,
        grid=(t // tm,),
        in_specs=[pl.BlockSpec((tm, D_MODEL), row),
                  pl.BlockSpec((tm, ATTN_W), row),
                  pl.BlockSpec((tm, CONV_W), row),
                  _resident(w_out_b.shape),
                  _resident((1, D_MODEL)),
                  _resident(w_query_b.shape)],
        out_specs=[pl.BlockSpec((tm, D_MODEL), row),
                   pl.BlockSpec((D_MODEL, tm), lambda i: (0, i)),
                   pl.BlockSpec((tm, D_MODEL), row)],
        out_shape=[jax.ShapeDtypeStruct((t, D_MODEL), F32),
                   jax.ShapeDtypeStruct((D_MODEL, t), BF16),
                   jax.ShapeDtypeStruct((t, D_MODEL), BF16)],
        compiler_params=_params("arbitrary"),
        name="out_proj",
    )(x2d, a, c, w_out_b, g, w_query_b)


_PAIR_WIDTHS = [P_TOPK // (a + 1) for a in range(P_TOPK)]
N_PAIRS = sum(_PAIR_WIDTHS)
PAIR_ROWS = -(-N_PAIRS // SUBLANES) * SUBLANES
HEAD_GROUP = 2


def _topk_rounds(xs, stable):
    xs = list(xs)
    vals = [[] for _ in xs]
    ranks = [None] * len(xs)
    if stable:
        ranks = [jnp.full(x.shape, float(P_TOPK), F32) for x in xs]
        rows = [lax.broadcasted_iota(jnp.int32, x.shape, 0).astype(F32) for x in xs]
    for r in range(P_TOPK):
        for a, x in enumerate(xs):
            m = jnp.max(x, axis=0, keepdims=True)
            if stable:
                first = jnp.min(jnp.where(x == m, rows[a], float(x.shape[0])),
                                axis=0, keepdims=True)
                hit = rows[a] == first
                ranks[a] = jnp.where(hit, float(r), ranks[a])
            else:
                hit = x == m
            xs[a] = jnp.where(hit, -jnp.inf, x)
            vals[a].append(m)
    return list(zip(vals, xs, ranks))


def _count(mask):
    return jnp.sum(jnp.where(mask, 1.0, 0.0), axis=0, keepdims=True)


def _peer_select_kernel(qp_ref, sk_ref, r2_ref, e2_ref, n1_ref, e1_ref,
                        t2_ref, cand_ref, pick_ref):
    tm = qp_ref.shape[0]
    k = float(P_TOPK)
    for slot in range(HEAD_GROUP):
        cand_ref[slot, N_PAIRS:, :] = jnp.full((PAIR_ROWS - N_PAIRS, tm), -jnp.inf, F32)

    def scores(h):
        def score(c):
            q = qp_ref[:, (2 * h + c) * N_KEYS:(2 * h + c + 1) * N_KEYS]
            return lax.dot_general(sk_ref[h, c], q, _NT, preferred_element_type=F32)
        return score(0), score(1)

    def pair_sums(top1, top2, slot):
        for r in range(P_TOPK):
            t2_ref[r:r + 1, :] = top2[r]
        pos = 0
        for a, width in enumerate(_PAIR_WIDTHS):
            cand_ref[slot, pos:pos + width, :] = top1[a] + t2_ref[0:width, :]
            pos += width
        return cand_ref[slot]

    def finish(h, s1, s2, top1, top2, cand, chosen, first_list_row, rank2):
        best = top1[0] + top2[0]
        z = jnp.sum(jnp.where(chosen, jnp.exp(cand - best), 0.0), axis=0, keepdims=True)
        pick_ref[...] = jnp.where(chosen, 1.0, 0.0)
        n1 = jnp.zeros_like(s1)
        pos = 0
        for a, width in enumerate(_PAIR_WIDTHS):
            count = jnp.sum(pick_ref[pos:pos + width, :], axis=0, keepdims=True)
            n1 = jnp.where(first_list_row(a), count, n1)
            pos += width
        r2_ref[h] = rank2.astype(BF16)
        e2_ref[h] = jnp.exp(s2 - top2[0]).astype(BF16)
        n1_ref[h] = n1
        e1_ref[h] = jnp.exp(s1 - top1[0]) / z

    def head_stable(h):
        s1, s2 = scores(h)
        (top1, _, rank1), (top2, _, rank2) = _topk_rounds([s1, s2], True)
        cand = pair_sums(top1, top2, 0)
        ((_, _, crank),) = _topk_rounds([cand], True)
        finish(h, s1, s2, top1, top2, cand, crank < k, lambda a: rank1 == float(a), rank2)

    n_groups = P_HEADS // HEAD_GROUP
    pending = []
    for g in range(n_groups + 1):
        fresh = [(h,) + scores(h) for h in range(g * HEAD_GROUP, (g + 1) * HEAD_GROUP)
                 if g < n_groups]
        xs = [s for (_, s1, s2) in fresh for s in (s1, s2)]
        cands = [pair_sums(r1[0], r2[0], slot) for slot, (_, _, _, r1, r2) in enumerate(pending)]
        res = _topk_rounds(xs + cands, False)
        for slot, (ph, ps1, ps2, (ptop1, pleft1, _), (ptop2, _, _)) in enumerate(pending):
            cand = cands[slot]
            ctop = res[len(xs) + slot][0]
            chosen = cand >= ctop[P_TOPK - 1]
            rank2 = jnp.zeros_like(ps2)
            for r in range(P_TOPK):
                rank2 = rank2 + jnp.where(ps2 < ptop2[r], 1.0, 0.0)
            finish(ph, ps1, ps2, ptop1, ptop2, cand, chosen,
                   lambda a, ps1=ps1, ptop1=ptop1: ps1 == ptop1[a], rank2)
            ties = (jnp.abs(_count(pleft1 == -jnp.inf) - k) + jnp.abs(_count(rank2 < k) - k)
                    + jnp.abs(_count(chosen) - k))

            @pl.when(jnp.max(ties) > 0.0)
            def _():
                head_stable(ph)
        pending = [(h, s1, s2, res[2 * n], res[2 * n + 1]) for n, (h, s1, s2) in enumerate(fresh)]


def _peer_select(qp, sub_keys_b, tm):
    t = qp.shape[0]
    blk = pl.BlockSpec((P_HEADS, N_KEYS, tm), lambda i: (0, 0, i))
    out = lambda dt: jax.ShapeDtypeStruct((P_HEADS, N_KEYS, t), dt)
    return pl.pallas_call(
        _peer_select_kernel,
        grid=(t // tm,),
        in_specs=[pl.BlockSpec((tm, D_MODEL), lambda i: (i, 0)),
                  _resident(sub_keys_b.shape)],
        out_specs=[blk] * 4,
        out_shape=[out(BF16), out(BF16), out(F32), out(F32)],
        scratch_shapes=[pltpu.VMEM((P_TOPK, tm), F32),
                        pltpu.VMEM((HEAD_GROUP, PAIR_ROWS, tm), F32),
                        pltpu.VMEM((PAIR_ROWS, tm), F32)],
        compiler_params=_params("arbitrary"),
        name="peer_select",
    )(qp, sub_keys_b)


def _gelu(h):
    return 0.5 * h * (1.0 + lax.erf(h * (2.0 ** -0.5)))


BF16_ROWS = 2 * SUBLANES


DENSE_BLOCK = 2 * N_KEYS


def _peer_dense_kernel(xnt_ref, x1_ref, u_ref, v_ref, r2_ref, e2_ref, n1_ref, e1_ref,
                       o_ref, *, te):
    j = pl.program_id(1)
    tm = xnt_ref.shape[1]
    zero = jnp.zeros((), BF16)

    @pl.when(j == 0)
    def _():
        o_ref[...] = x1_ref[...]

    def token_row(ref, h, i1):
        row = jnp.broadcast_to(ref[h, pl.ds(i1, 1), :], (BF16_ROWS, tm))
        return row.astype(BF16)[None]

    def hidden(b, part):
        rows = slice(b * DENSE_BLOCK, (b + 1) * DENSE_BLOCK)
        cols = slice(part * (tm // n_parts), (part + 1) * (tm // n_parts))
        return jnp.dot(u_ref[rows, :], xnt_ref[:, cols], preferred_element_type=F32)

    n_blocks = te // DENSE_BLOCK
    n_parts = DENSE_BLOCK // N_KEYS
    ht_next = [hidden(0, part) for part in range(n_parts)]
    parts = []
    for b in range(n_blocks):
        ht = jnp.concatenate(ht_next, axis=-1)
        ht_next = []
        for kb in range(n_parts):
            if b + 1 < n_blocks:
                ht_next.append(hidden(b + 1, kb))
            i1 = j * (te // N_KEYS) + b * n_parts + kb
            gate = None
            for h in range(P_HEADS):
                term = token_row(e1_ref, h, i1) * jnp.where(
                    r2_ref[h] < token_row(n1_ref, h, i1), e2_ref[h], zero)
                gate = term if gate is None else gate + term
            act = _gelu(ht[kb * N_KEYS:(kb + 1) * N_KEYS].astype(BF16)).reshape(gate.shape)
            parts.append((act * gate).reshape(N_KEYS, tm))
    at = jnp.concatenate(parts, axis=0)
    o_ref[...] += lax.dot_general(at, v_ref[...], _TN, preferred_element_type=F32)


def _peer_dense(xnt, x1, eu_b, ev_b, r2, e2, n1, e1, tm, te):
    t = x1.shape[0]
    n_exp = eu_b.shape[0]
    once = pl.Buffered(1)
    tok = lambda i, j: (i, 0)
    exp = pl.BlockSpec((te, D_MODEL), lambda i, j: (j, 0))
    packed = pl.BlockSpec((P_HEADS, N_KEYS // BF16_ROWS, BF16_ROWS, tm),
                          lambda i, j: (0, 0, 0, i), pipeline_mode=once)
    rows = pl.BlockSpec((P_HEADS, N_KEYS, tm), lambda i, j: (0, 0, i), pipeline_mode=once)
    split = lambda y: y.reshape(P_HEADS, N_KEYS // BF16_ROWS, BF16_ROWS, t)
    return pl.pallas_call(
        functools.partial(_peer_dense_kernel, te=te),
        grid=(t // tm, n_exp // te),
        in_specs=[pl.BlockSpec((D_MODEL, tm), lambda i, j: (0, i), pipeline_mode=once),
                  pl.BlockSpec((tm, D_MODEL), tok, pipeline_mode=once),
                  exp, exp, packed, packed, rows, rows],
        out_specs=pl.BlockSpec((tm, D_MODEL), tok),
        out_shape=jax.ShapeDtypeStruct((t, D_MODEL), F32),
        compiler_params=_params("arbitrary", "arbitrary"),
        name="peer_dense",
    )(xnt, x1, eu_b, ev_b, split(r2), split(e2), n1, e1)


def _tiles():
    return dict(proj_rows=256, attn_q=512, conv_rows=256, select_tokens=256,
                dense_tokens=512, dense_experts=1024, sample_pages=16, sample_rows=128)


def _block_ones(width, group):
    r = jnp.arange(width) // group
    return (r[:, None] == r[None, :]).astype(BF16)


def _finish(x2d, a, c, w_out_b, norm_ffn_g, w_query_b, sub_keys_b, eu_b, ev_b, tiles, tm):
    x1, xnt, qp = _out_proj(x2d, a, c, w_out_b, norm_ffn_g, w_query_b, tm)
    sel = _peer_select(qp, sub_keys_b, min(tiles["select_tokens"], x2d.shape[0]))
    return _peer_dense(xnt, x1, eu_b, ev_b, *sel,
                       min(tiles["dense_tokens"], x2d.shape[0]), tiles["dense_experts"])


def kernel(x_prompt, x_sample, cache_k, cache_v, state_conv, page_table, norm_mix_g, w_in,
           q_norm_g, k_norm_g, lambda_q1, lambda_k1, lambda_q2, lambda_k2, subln_g, w_dw, b_dw,
           conv_norm_g, w_out, norm_ffn_g, w_query, sub_keys, expert_u, expert_v):
    tiles = _tiles()
    b, s, _ = x_prompt.shape
    bd = x_sample.shape[0]
    assert x_sample.shape[1] == 1

    w_in_b = w_in.astype(BF16)
    w_out_b = w_out.astype(BF16)
    w_query_b = w_query.astype(BF16)
    sub_keys_b = sub_keys.astype(BF16)
    eu_b = expert_u.astype(BF16)
    ev_b = expert_v.astype(BF16)
    row = lambda v: v.reshape(1, -1)
    qg = row(jnp.tile(q_norm_g, ATTN_W // HEAD_DIM))
    kg = row(jnp.tile(k_norm_g, ATTN_W // HEAD_DIM))
    lam4 = jnp.stack([lambda_q1, lambda_k1, lambda_q2, lambda_k2])
    pmat = _block_ones(2 * LANES, HEAD_DIM)
    sg, bdw, cg = row(subln_g), row(b_dw), row(conv_norm_g)
    nmix, nffn = row(norm_mix_g), row(norm_ffn_g)

    xp = x_prompt.reshape(b * s, D_MODEL)
    qb, k, kb, v, vb, u = _in_proj(xp, nmix, w_in_b, qg, kg, pmat, tiles["proj_rows"])
    r3 = lambda y: y.reshape(b, s, -1)
    a_p = _attn_prompt(lam4, sg, r3(qb), r3(kb), r3(vb), tiles["attn_q"])
    c_p = _conv_prompt(r3(u), w_dw, bdw, cg, tiles["conv_rows"])
    y_p = _finish(xp, a_p.reshape(b * s, ATTN_W), c_p.reshape(b * s, CONV_W), w_out_b, nffn,
                  w_query_b, sub_keys_b, eu_b, ev_b, tiles, tiles["proj_rows"])

    rows = tiles["sample_rows"]
    xs = jnp.pad(x_sample.reshape(bd, D_MODEL), ((0, rows - bd), (0, 0)))
    qs, ks, _, vs, _, us = _in_proj(xs, nmix, w_in_b, qg, kg, pmat, rows)
    ks, vs, us = ks[:bd], vs[:bd], us[:bd]
    h3 = lambda y: y.reshape(bd, N_HEADS, HEAD_W)
    a_s = _attn_sample(page_table, lam4, sg, h3(qs[:bd].astype(F32)), h3(ks), h3(vs),
                       cache_k, cache_v, tiles["sample_pages"])
    c_s = _conv_sample(state_conv, us, w_dw, bdw, cg)
    pad = lambda y: jnp.pad(y, ((0, rows - bd), (0, 0)))
    y_s = _finish(xs, pad(a_s.reshape(bd, ATTN_W)), pad(c_s), w_out_b, nffn,
                  w_query_b, sub_keys_b, eu_b, ev_b, tiles, rows)[:bd]

    heads = lambda y, n: y.reshape(n, -1, N_HEADS, HEAD_W)
    conv_prompt = r3(u)[:, s - CONV_STATE:]
    conv_sample = jnp.concatenate([state_conv[:, 1:], us[:, None, :]], axis=1)
    return (y_p.reshape(b, s, D_MODEL), y_s.reshape(bd, 1, D_MODEL),
            heads(k, b), heads(v, b), conv_prompt,
            heads(ks, bd), heads(vs, bd), conv_sample)
```

```python
import functools
import math

import jax
import jax.numpy as jnp
from jax import lax
from jax.experimental import pallas as pl
from jax.experimental.pallas import tpu as pltpu

F32 = jnp.float32
BF16 = jnp.bfloat16

D_MODEL = 2048
N_HEADS = 8
HEAD_DIM = 64
HEAD_W = 2 * HEAD_DIM
ATTN_W = N_HEADS * HEAD_W
CONV_W = D_MODEL - ATTN_W
CONV_WIDTH = 31
CONV_STATE = CONV_WIDTH - 1
LAMBDA_INIT = 0.8 - 0.6 * math.exp(-0.3 * (1 - 1))
SCALE = HEAD_DIM ** -0.5
PAGE_SIZE = 128
N_KEYS = 128
P_HEADS = 8
P_TOPK = 16
EPS = 1e-6
NEG = -1e30
LOG2E = math.log2(math.e)

SUBLANES = 8
LANES = 128
VMEM_LIMIT_BYTES = 56 * 1024 * 1024

_NT = (((1,), (1,)), ((), ()))
_TN = (((0,), (0,)), ((), ()))


def _params(*sem):
    return pltpu.CompilerParams(dimension_semantics=sem, vmem_limit_bytes=VMEM_LIMIT_BYTES)


def _resident(shape):
    return pl.BlockSpec(shape, lambda *_: (0,) * len(shape), pipeline_mode=pl.Buffered(1))


def _rms(x, gain):
    ms = jnp.mean(x * x, axis=-1, keepdims=True)
    return x * lax.rsqrt(ms + EPS) * gain


def _sigmoid(x):
    return 1.0 / (1.0 + jnp.exp(-x))


def _diff_lambda(lam_ref):
    l = lam_ref[...]
    e1 = jnp.exp(jnp.sum(l[0:1] * l[1:2], axis=-1, keepdims=True))
    e2 = jnp.exp(jnp.sum(l[2:3] * l[3:4], axis=-1, keepdims=True))
    return e1 - e2 + LAMBDA_INIT


def _in_proj_kernel(x_ref, g_ref, w_ref, qg_ref, kg_ref, p_ref,
                    qb_ref, k_ref, kb_ref, v_ref, vb_ref, u_ref):
    xn = _rms(x_ref[...], g_ref[...]).astype(BF16)

    def proj(col):
        return jnp.dot(xn, w_ref[:, col:col + ATTN_W], preferred_element_type=F32)

    def head_norm(y, gain):
        y2 = y * y
        hi = y2.astype(BF16)
        lo = (y2 - hi.astype(F32)).astype(BF16)
        p = p_ref[...]
        w = p.shape[0]
        parts = []
        for c in range(ATTN_W // w):
            sl = slice(c * w, (c + 1) * w)
            parts.append(jnp.dot(hi[:, sl], p, preferred_element_type=F32)
                         + jnp.dot(lo[:, sl], p, preferred_element_type=F32))
        ss = jnp.concatenate(parts, axis=-1)
        return y * lax.rsqrt(ss * (1.0 / HEAD_DIM) + EPS) * gain

    q = head_norm(proj(0), qg_ref[...])
    qb_ref[...] = (q * (SCALE * LOG2E)).astype(BF16)
    k = head_norm(proj(ATTN_W), kg_ref[...])
    k_ref[...] = k
    kb_ref[...] = k.astype(BF16)
    v = proj(2 * ATTN_W)
    v_ref[...] = v
    vb_ref[...] = v.astype(BF16)
    a = proj(3 * ATTN_W)
    gt = proj(3 * ATTN_W + CONV_W)
    u_ref[...] = a * _sigmoid(gt)


def _in_proj(x2d, norm_g, w_in_b, qg, kg, pmat, tm):
    t = x2d.shape[0]
    row = lambda i: (i, 0)
    blk = lambda: pl.BlockSpec((tm, ATTN_W), row)
    return pl.pallas_call(
        _in_proj_kernel,
        grid=(t // tm,),
        in_specs=[pl.BlockSpec((tm, D_MODEL), row),
                  _resident((1, D_MODEL)),
                  _resident(w_in_b.shape),
                  _resident((1, ATTN_W)),
                  _resident((1, ATTN_W)),
                  _resident(pmat.shape)],
        out_specs=[blk(), blk(), blk(), blk(), blk(), blk()],
        out_shape=[jax.ShapeDtypeStruct((t, ATTN_W), BF16),
                   jax.ShapeDtypeStruct((t, ATTN_W), F32),
                   jax.ShapeDtypeStruct((t, ATTN_W), BF16),
                   jax.ShapeDtypeStruct((t, ATTN_W), F32),
                   jax.ShapeDtypeStruct((t, ATTN_W), BF16),
                   jax.ShapeDtypeStruct((t, CONV_W), F32)],
        compiler_params=_params("arbitrary"),
        name="in_proj",
    )(x2d, norm_g, w_in_b, qg, kg, pmat)


def _lane_fold(x, op):
    out = x[:, :LANES]
    for c in range(1, x.shape[1] // LANES):
        out = op(out, x[:, c * LANES:(c + 1) * LANES])
    return out


def _attn_prompt_kernel(lam_ref, sg_ref, q_ref, k_ref, v_ref, o_ref, s1_ref, s2_ref, *, tq):
    lam = _diff_lambda(lam_ref)
    seq = q_ref.shape[1]
    lane = lax.broadcasted_iota(jnp.int32, (tq, HEAD_W), 1)
    r = lax.broadcasted_iota(jnp.int32, (tq, tq), 0)
    c = lax.broadcasted_iota(jnp.int32, (tq, tq), 1)
    keep = c <= r
    for i in range(seq // tq):
        rows = slice(i * tq, (i + 1) * tq)
        q = q_ref[0, rows, :]
        zero = jnp.zeros_like(q)
        qs = (jnp.where(lane < HEAD_DIM, q, zero), jnp.where(lane >= HEAD_DIM, q, zero))
        tops = [jnp.full((tq, LANES), NEG, F32) for _ in qs]
        for j in range(i + 1):
            cols = slice(j * tq, (j + 1) * tq)
            kj = k_ref[0, cols, :]
            for n, s_ref in enumerate((s1_ref, s2_ref)):
                s = lax.dot_general(qs[n], kj, _NT, preferred_element_type=F32)
                if j == i:
                    s = jnp.where(keep, s, NEG)
                s_ref[:, cols] = s
                tops[n] = jnp.maximum(tops[n], _lane_fold(s, jnp.maximum))
        outs = []
        for n, s_ref in enumerate((s1_ref, s2_ref)):
            m = jnp.max(tops[n], axis=-1, keepdims=True)
            part = jnp.zeros((tq, LANES), F32)
            acc = jnp.zeros((tq, HEAD_W), F32)
            for j in range(i + 1):
                cols = slice(j * tq, (j + 1) * tq)
                p = jnp.exp2(s_ref[:, cols] - m)
                part = part + _lane_fold(p, jnp.add)
                acc = acc + jnp.dot(p.astype(BF16), v_ref[0, cols, :],
                                    preferred_element_type=F32)
            outs.append(acc / jnp.sum(part, axis=-1, keepdims=True))
        o = outs[0] - lam * outs[1]
        o_ref[0, rows, :] = (_rms(o, sg_ref[...]) * (1.0 - LAMBDA_INIT)).astype(o_ref.dtype)


def _attn_prompt(lam4, subln_g, qb, kb, vb, tq):
    b, s, _ = qb.shape
    head = pl.BlockSpec((1, s, HEAD_W), lambda bi, h: (bi, 0, h))
    return pl.pallas_call(
        functools.partial(_attn_prompt_kernel, tq=tq),
        grid=(b, N_HEADS),
        in_specs=[_resident(lam4.shape), _resident((1, HEAD_W)), head, head, head],
        out_specs=head,
        out_shape=jax.ShapeDtypeStruct((b, s, ATTN_W), BF16),
        scratch_shapes=[pltpu.VMEM((tq, s), F32), pltpu.VMEM((tq, s), F32)],
        compiler_params=_params("arbitrary", "arbitrary"),
        name="attn_prompt",
    )(lam4, subln_g, qb, kb, vb)


def _attn_sample_kernel(pt_ref, lam_ref, sg_ref, q_ref, kn_ref, vn_ref, *rest, pages):
    k_refs = rest[:pages]
    v_refs = rest[pages:2 * pages]
    o_ref = rest[2 * pages]
    m_ref, l_ref, acc_ref = rest[2 * pages + 1:]
    p = pl.program_id(1)
    q = q_ref[0]
    lane = lax.broadcasted_iota(jnp.int32, q.shape, 1)
    zero = jnp.zeros_like(q)
    q_rows = jnp.concatenate([jnp.where(lane < HEAD_DIM, q, zero),
                              jnp.where(lane >= HEAD_DIM, q, zero)], axis=0)
    cols = PAGE_SIZE * N_HEADS
    row_head = lax.broadcasted_iota(jnp.int32, (2 * N_HEADS, cols), 0) % N_HEADS
    col_head = lax.broadcasted_iota(jnp.int32, (2 * N_HEADS, cols), 1) % N_HEADS
    own_head = row_head == col_head

    @pl.when(p == 0)
    def _():
        kn = kn_ref[0]
        vn = vn_ref[0]
        s_new = jnp.sum(q_rows * jnp.concatenate([kn, kn], axis=0), axis=-1, keepdims=True)
        m_ref[...] = jnp.broadcast_to(s_new, m_ref.shape)
        l_ref[...] = jnp.ones_like(l_ref)
        acc_ref[...] = jnp.concatenate([vn, vn], axis=0)

    m = m_ref[:, 0:1]
    l = l_ref[:, 0:1]
    qb = q_rows.astype(BF16)
    scores = []
    for r in range(pages):
        kb = k_refs[r][0].reshape(cols, HEAD_W).astype(BF16)
        s = lax.dot_general(qb, kb, _NT, preferred_element_type=F32)
        scores.append(jnp.where(own_head, s, NEG))
    top = scores[0]
    for s in scores[1:]:
        top = jnp.maximum(top, s)
    m_new = jnp.maximum(m, jnp.max(top, axis=-1, keepdims=True))
    corr = jnp.exp2(m - m_new)
    l = l * corr
    acc = acc_ref[...] * corr
    for r in range(pages):
        pr = jnp.exp2(scores[r] - m_new)
        l = l + jnp.sum(pr, axis=-1, keepdims=True)
        vb = v_refs[r][0].reshape(cols, HEAD_W).astype(BF16)
        acc = acc + jnp.dot(pr.astype(BF16), vb, preferred_element_type=F32)
    m_ref[...] = jnp.broadcast_to(m_new, m_ref.shape)
    l_ref[...] = jnp.broadcast_to(l, l_ref.shape)
    acc_ref[...] = acc

    @pl.when(p == pl.num_programs(1) - 1)
    def _():
        o = acc_ref[...] / l_ref[...]
        w = o[:N_HEADS] - _diff_lambda(lam_ref) * o[N_HEADS:]
        o_ref[0] = (_rms(w, sg_ref[...]) * (1.0 - LAMBDA_INIT)).astype(o_ref.dtype)


def _attn_sample(page_table, lam4, subln_g, q, k_new, v_new, cache_k, cache_v, pages):
    bd, n_pages = page_table.shape
    page_blk = (1, PAGE_SIZE, N_HEADS, HEAD_W)

    def page_spec(r):
        return pl.BlockSpec(page_blk, lambda b, p, pt: (pt[b, p * pages + r], 0, 0, 0))

    tok = pl.BlockSpec((1, N_HEADS, HEAD_W), lambda b, p, pt: (b, 0, 0))
    const = lambda shape: pl.BlockSpec(shape, lambda b, p, pt: (0,) * len(shape))
    state = pltpu.VMEM((2 * N_HEADS, HEAD_W), F32)
    grid_spec = pltpu.PrefetchScalarGridSpec(
        num_scalar_prefetch=1,
        grid=(bd, n_pages // pages),
        in_specs=[const(lam4.shape), const((1, HEAD_W)), tok, tok, tok]
                 + [page_spec(r) for r in range(pages)]
                 + [page_spec(r) for r in range(pages)],
        out_specs=tok,
        scratch_shapes=[state, state, state],
    )
    return pl.pallas_call(
        functools.partial(_attn_sample_kernel, pages=pages),
        grid_spec=grid_spec,
        out_shape=jax.ShapeDtypeStruct((bd, N_HEADS, HEAD_W), BF16),
        compiler_params=_params("arbitrary", "arbitrary"),
        name="attn_sample",
    )(page_table, lam4, subln_g, q, k_new, v_new,
      *([cache_k] * pages), *([cache_v] * pages))


CONV_HALO = 32
CONV_ROWS = 64


def _conv_finish(y, g_ref):
    c = _rms(y, g_ref[...])
    return (c * _sigmoid(c)).astype(BF16)


def _conv_prompt_kernel(u_ref, halo_ref, w_ref, b_ref, g_ref, o_ref, win_ref, sh_ref, y_ref,
                        *, ts):
    i = pl.program_id(1)
    halo = halo_ref[0]
    win_ref[0:CONV_HALO, :] = jnp.where(i == 0, jnp.zeros_like(halo), halo)
    win_ref[CONV_HALO:, :] = u_ref[0]
    shift = CONV_HALO - CONV_STATE
    span = sh_ref.shape[1]

    def lane_chunk(c, _):
        lanes = pl.ds(pl.multiple_of(c * LANES, LANES), LANES)
        for phase in range(1, SUBLANES):
            sh_ref[phase] = win_ref[phase:phase + span, lanes]
        bias = b_ref[:, lanes]
        for r in range(ts // CONV_ROWS):
            acc = jnp.zeros((CONV_ROWS, LANES), F32) + bias
            for j in range(CONV_WIDTH):
                phase = (j + shift) % SUBLANES
                row0 = r * CONV_ROWS + j + shift - phase
                if phase == 0:
                    rows = win_ref[row0:row0 + CONV_ROWS, lanes]
                else:
                    rows = sh_ref[phase, row0:row0 + CONV_ROWS, :]
                acc = acc + w_ref[j:j + 1, lanes] * rows
            y_ref[r * CONV_ROWS:(r + 1) * CONV_ROWS, lanes] = acc
        return 0

    lax.fori_loop(0, CONV_W // LANES, lane_chunk, 0)
    o_ref[0] = _conv_finish(y_ref[...], g_ref)


def _conv_prompt(u, w_dw, b_dw, g, ts):
    b, s, _ = u.shape
    per = ts // CONV_HALO
    return pl.pallas_call(
        functools.partial(_conv_prompt_kernel, ts=ts),
        grid=(b, s // ts),
        in_specs=[pl.BlockSpec((1, ts, CONV_W), lambda bi, i: (bi, i, 0)),
                  pl.BlockSpec((1, CONV_HALO, CONV_W),
                               lambda bi, i: (bi, jnp.maximum(i * per - 1, 0), 0)),
                  _resident(w_dw.shape),
                  _resident((1, CONV_W)),
                  _resident((1, CONV_W))],
        out_specs=pl.BlockSpec((1, ts, CONV_W), lambda bi, i: (bi, i, 0)),
        out_shape=jax.ShapeDtypeStruct((b, s, CONV_W), BF16),
        scratch_shapes=[pltpu.VMEM((CONV_HALO + ts, CONV_W), F32),
                        pltpu.VMEM((SUBLANES, CONV_HALO + ts - SUBLANES, LANES), F32),
                        pltpu.VMEM((ts, CONV_W), F32)],
        compiler_params=_params("arbitrary", "arbitrary"),
        name="conv_prompt",
    )(u, u, w_dw, b_dw, g)


def _conv_sample_kernel(st_ref, u_ref, w_ref, b_ref, g_ref, o_ref):
    acc = b_ref[...] + w_ref[CONV_STATE:CONV_WIDTH, :] * u_ref[...]
    for j in range(CONV_STATE):
        acc = acc + w_ref[j:j + 1, :] * st_ref[:, j, :]
    o_ref[...] = _conv_finish(acc, g_ref)


def _conv_sample(state, u, w_dw, b_dw, g):
    bd = u.shape[0]
    return pl.pallas_call(
        _conv_sample_kernel,
        out_shape=jax.ShapeDtypeStruct((bd, CONV_W), BF16),
        compiler_params=pltpu.CompilerParams(vmem_limit_bytes=VMEM_LIMIT_BYTES),
        name="conv_sample",
    )(state, u, w_dw, b_dw, g)


def _out_proj_kernel(x_ref, a_ref, c_ref, wo_ref, g_ref, wq_ref, x1_ref, xnt_ref, qp_ref):
    x1 = (x_ref[...]
          + jnp.dot(a_ref[...], wo_ref[0:ATTN_W, :], preferred_element_type=F32)
          + jnp.dot(c_ref[...], wo_ref[ATTN_W:D_MODEL, :], preferred_element_type=F32))
    x1_ref[...] = x1
    xn = _rms(x1, g_ref[...])
    xnt_ref[...] = xn.T.astype(BF16)
    qp_ref[...] = jnp.dot(xn.astype(BF16), wq_ref[...],
                          preferred_element_type=F32).astype(BF16)


def _out_proj(x2d, a, c, w_out_b, g, w_query_b, tm):
    t = x2d.shape[0]
    row = lambda i: (i, 0)
    return pl.pallas_call(
        _out_proj_kernel,
        grid=(t // tm,),
        in_specs=[pl.BlockSpec((tm, D_MODEL), row),
                  pl.BlockSpec((tm, ATTN_W), row),
                  pl.BlockSpec((tm, CONV_W), row),
                  _resident(w_out_b.shape),
                  _resident((1, D_MODEL)),
                  _resident(w_query_b.shape)],
        out_specs=[pl.BlockSpec((tm, D_MODEL), row),
                   pl.BlockSpec((D_MODEL, tm), lambda i: (0, i)),
                   pl.BlockSpec((tm, D_MODEL), row)],
        out_shape=[jax.ShapeDtypeStruct((t, D_MODEL), F32),
                   jax.ShapeDtypeStruct((D_MODEL, t), BF16),
                   jax.ShapeDtypeStruct((t, D_MODEL), BF16)],
        compiler_params=_params("arbitrary"),
        name="out_proj",
    )(x2d, a, c, w_out_b, g, w_query_b)


_PAIR_WIDTHS = [P_TOPK // (a + 1) for a in range(P_TOPK)]
N_PAIRS = sum(_PAIR_WIDTHS)
PAIR_ROWS = -(-N_PAIRS // SUBLANES) * SUBLANES
HEAD_GROUP = 2


def _topk_rounds(xs, stable):
    xs = list(xs)
    vals = [[] for _ in xs]
    ranks = [None] * len(xs)
    if stable:
        ranks = [jnp.full(x.shape, float(P_TOPK), F32) for x in xs]
        rows = [lax.broadcasted_iota(jnp.int32, x.shape, 0).astype(F32) for x in xs]
    for r in range(P_TOPK):
        for a, x in enumerate(xs):
            m = jnp.max(x, axis=0, keepdims=True)
            if stable:
                first = jnp.min(jnp.where(x == m, rows[a], float(x.shape[0])),
                                axis=0, keepdims=True)
                hit = rows[a] == first
                ranks[a] = jnp.where(hit, float(r), ranks[a])
            else:
                hit = x == m
            xs[a] = jnp.where(hit, -jnp.inf, x)
            vals[a].append(m)
    return list(zip(vals, xs, ranks))


def _count(mask):
    return jnp.sum(jnp.where(mask, 1.0, 0.0), axis=0, keepdims=True)


def _peer_select_kernel(qp_ref, sk_ref, r2_ref, e2_ref, n1_ref, e1_ref,
                        t2_ref, cand_ref, pick_ref):
    tm = qp_ref.shape[0]
    k = float(P_TOPK)
    for slot in range(HEAD_GROUP):
        cand_ref[slot, N_PAIRS:, :] = jnp.full((PAIR_ROWS - N_PAIRS, tm), -jnp.inf, F32)

    def scores(h):
        def score(c):
            q = qp_ref[:, (2 * h + c) * N_KEYS:(2 * h + c + 1) * N_KEYS]
            return lax.dot_general(sk_ref[h, c], q, _NT, preferred_element_type=F32)
        return score(0), score(1)

    def pair_sums(top1, top2, slot):
        for r in range(P_TOPK):
            t2_ref[r:r + 1, :] = top2[r]
        pos = 0
        for a, width in enumerate(_PAIR_WIDTHS):
            cand_ref[slot, pos:pos + width, :] = top1[a] + t2_ref[0:width, :]
            pos += width
        return cand_ref[slot]

    def finish(h, s1, s2, top1, top2, cand, chosen, first_list_row, rank2):
        best = top1[0] + top2[0]
        z = jnp.sum(jnp.where(chosen, jnp.exp(cand - best), 0.0), axis=0, keepdims=True)
        pick_ref[...] = jnp.where(chosen, 1.0, 0.0)
        n1 = jnp.zeros_like(s1)
        pos = 0
        for a, width in enumerate(_PAIR_WIDTHS):
            count = jnp.sum(pick_ref[pos:pos + width, :], axis=0, keepdims=True)
            n1 = jnp.where(first_list_row(a), count, n1)
            pos += width
        r2_ref[h] = rank2.astype(BF16)
        e2_ref[h] = jnp.exp(s2 - top2[0]).astype(BF16)
        n1_ref[h] = n1
        e1_ref[h] = jnp.exp(s1 - top1[0]) / z

    def head_stable(h):
        s1, s2 = scores(h)
        (top1, _, rank1), (top2, _, rank2) = _topk_rounds([s1, s2], True)
        cand = pair_sums(top1, top2, 0)
        ((_, _, crank),) = _topk_rounds([cand], True)
        finish(h, s1, s2, top1, top2, cand, crank < k, lambda a: rank1 == float(a), rank2)

    n_groups = P_HEADS // HEAD_GROUP
    pending = []
    for g in range(n_groups + 1):
        fresh = [(h,) + scores(h) for h in range(g * HEAD_GROUP, (g + 1) * HEAD_GROUP)
                 if g < n_groups]
        xs = [s for (_, s1, s2) in fresh for s in (s1, s2)]
        cands = [pair_sums(r1[0], r2[0], slot) for slot, (_, _, _, r1, r2) in enumerate(pending)]
        res = _topk_rounds(xs + cands, False)
        for slot, (ph, ps1, ps2, (ptop1, pleft1, _), (ptop2, _, _)) in enumerate(pending):
            cand = cands[slot]
            ctop = res[len(xs) + slot][0]
            chosen = cand >= ctop[P_TOPK - 1]
            rank2 = jnp.zeros_like(ps2)
            for r in range(P_TOPK):
                rank2 = rank2 + jnp.where(ps2 < ptop2[r], 1.0, 0.0)
            finish(ph, ps1, ps2, ptop1, ptop2, cand, chosen,
                   lambda a, ps1=ps1, ptop1=ptop1: ps1 == ptop1[a], rank2)
            ties = (jnp.abs(_count(pleft1 == -jnp.inf) - k) + jnp.abs(_count(rank2 < k) - k)
                    + jnp.abs(_count(chosen) - k))

            @pl.when(jnp.max(ties) > 0.0)
            def _():
                head_stable(ph)
        pending = [(h, s1, s2, res[2 * n], res[2 * n + 1]) for n, (h, s1, s2) in enumerate(fresh)]


def _peer_select(qp, sub_keys_b, tm):
    t = qp.shape[0]
    blk = pl.BlockSpec((P_HEADS, N_KEYS, tm), lambda i: (0, 0, i))
    out = lambda dt: jax.ShapeDtypeStruct((P_HEADS, N_KEYS, t), dt)
    return pl.pallas_call(
        _peer_select_kernel,
        grid=(t // tm,),
        in_specs=[pl.BlockSpec((tm, D_MODEL), lambda i: (i, 0)),
                  _resident(sub_keys_b.shape)],
        out_specs=[blk] * 4,
        out_shape=[out(BF16), out(BF16), out(F32), out(F32)],
        scratch_shapes=[pltpu.VMEM((P_TOPK, tm), F32),
                        pltpu.VMEM((HEAD_GROUP, PAIR_ROWS, tm), F32),
                        pltpu.VMEM((PAIR_ROWS, tm), F32)],
        compiler_params=_params("arbitrary"),
        name="peer_select",
    )(qp, sub_keys_b)


def _gelu(h):
    return 0.5 * h * (1.0 + lax.erf(h * (2.0 ** -0.5)))


BF16_ROWS = 2 * SUBLANES


DENSE_BLOCK = 2 * N_KEYS


def _peer_dense_kernel(xnt_ref, x1_ref, u_ref, v_ref, r2_ref, e2_ref, n1_ref, e1_ref,
                       o_ref, *, te):
    j = pl.program_id(1)
    tm = xnt_ref.shape[1]
    zero = jnp.zeros((), BF16)

    @pl.when(j == 0)
    def _():
        o_ref[...] = x1_ref[...]

    def token_row(ref, h, i1):
        row = jnp.broadcast_to(ref[h, pl.ds(i1, 1), :], (BF16_ROWS, tm))
        return row.astype(BF16)[None]

    def hidden(b, part):
        rows = slice(b * DENSE_BLOCK, (b + 1) * DENSE_BLOCK)
        cols = slice(part * (tm // n_parts), (part + 1) * (tm // n_parts))
        return jnp.dot(u_ref[rows, :], xnt_ref[:, cols], preferred_element_type=F32)

    n_blocks = te // DENSE_BLOCK
    n_parts = DENSE_BLOCK // N_KEYS
    ht_next = [hidden(0, part) for part in range(n_parts)]
    parts = []
    for b in range(n_blocks):
        ht = jnp.concatenate(ht_next, axis=-1)
        ht_next = []
        for kb in range(n_parts):
            if b + 1 < n_blocks:
                ht_next.append(hidden(b + 1, kb))
            i1 = j * (te // N_KEYS) + b * n_parts + kb
            gate = None
            for h in range(P_HEADS):
                term = token_row(e1_ref, h, i1) * jnp.where(
                    r2_ref[h] < token_row(n1_ref, h, i1), e2_ref[h], zero)
                gate = term if gate is None else gate + term
            act = _gelu(ht[kb * N_KEYS:(kb + 1) * N_KEYS].astype(BF16)).reshape(gate.shape)
            parts.append((act * gate).reshape(N_KEYS, tm))
    at = jnp.concatenate(parts, axis=0)
    o_ref[...] += lax.dot_general(at, v_ref[...], _TN, preferred_element_type=F32)


def _peer_dense(xnt, x1, eu_b, ev_b, r2, e2, n1, e1, tm, te):
    t = x1.shape[0]
    n_exp = eu_b.shape[0]
    once = pl.Buffered(1)
    tok = lambda i, j: (i, 0)
    exp = pl.BlockSpec((te, D_MODEL), lambda i, j: (j, 0))
    packed = pl.BlockSpec((P_HEADS, N_KEYS // BF16_ROWS, BF16_ROWS, tm),
                          lambda i, j: (0, 0, 0, i), pipeline_mode=once)
    rows = pl.BlockSpec((P_HEADS, N_KEYS, tm), lambda i, j: (0, 0, i), pipeline_mode=once)
    split = lambda y: y.reshape(P_HEADS, N_KEYS // BF16_ROWS, BF16_ROWS, t)
    return pl.pallas_call(
        functools.partial(_peer_dense_kernel, te=te),
        grid=(t // tm, n_exp // te),
        in_specs=[pl.BlockSpec((D_MODEL, tm), lambda i, j: (0, i), pipeline_mode=once),
                  pl.BlockSpec((tm, D_MODEL), tok, pipeline_mode=once),
                  exp, exp, packed, packed, rows, rows],
        out_specs=pl.BlockSpec((tm, D_MODEL), tok),
        out_shape=jax.ShapeDtypeStruct((t, D_MODEL), F32),
        compiler_params=_params("arbitrary", "arbitrary"),
        name="peer_dense",
    )(xnt, x1, eu_b, ev_b, split(r2), split(e2), n1, e1)


def _tiles():
    return dict(proj_rows=256, attn_q=512, conv_rows=256, select_tokens=256,
                dense_tokens=1024, dense_experts=512, sample_pages=16, sample_rows=128)


def _block_ones(width, group):
    r = jnp.arange(width) // group
    return (r[:, None] == r[None, :]).astype(BF16)


def _finish(x2d, a, c, w_out_b, norm_ffn_g, w_query_b, sub_keys_b, eu_b, ev_b, tiles, tm):
    x1, xnt, qp = _out_proj(x2d, a, c, w_out_b, norm_ffn_g, w_query_b, tm)
    sel = _peer_select(qp, sub_keys_b, min(tiles["select_tokens"], x2d.shape[0]))
    return _peer_dense(xnt, x1, eu_b, ev_b, *sel,
                       min(tiles["dense_tokens"], x2d.shape[0]), tiles["dense_experts"])


def kernel(x_prompt, x_sample, cache_k, cache_v, state_conv, page_table, norm_mix_g, w_in,
           q_norm_g, k_norm_g, lambda_q1, lambda_k1, lambda_q2, lambda_k2, subln_g, w_dw, b_dw,
           conv_norm_g, w_out, norm_ffn_g, w_query, sub_keys, expert_u, expert_v):
    tiles = _tiles()
    b, s, _ = x_prompt.shape
    bd = x_sample.shape[0]
    assert x_sample.shape[1] == 1

    w_in_b = w_in.astype(BF16)
    w_out_b = w_out.astype(BF16)
    w_query_b = w_query.astype(BF16)
    sub_keys_b = sub_keys.astype(BF16)
    eu_b = expert_u.astype(BF16)
    ev_b = expert_v.astype(BF16)
    row = lambda v: v.reshape(1, -1)
    qg = row(jnp.tile(q_norm_g, ATTN_W // HEAD_DIM))
    kg = row(jnp.tile(k_norm_g, ATTN_W // HEAD_DIM))
    lam4 = jnp.stack([lambda_q1, lambda_k1, lambda_q2, lambda_k2])
    pmat = _block_ones(2 * LANES, HEAD_DIM)
    sg, bdw, cg = row(subln_g), row(b_dw), row(conv_norm_g)
    nmix, nffn = row(norm_mix_g), row(norm_ffn_g)

    xp = x_prompt.reshape(b * s, D_MODEL)
    qb, k, kb, v, vb, u = _in_proj(xp, nmix, w_in_b, qg, kg, pmat, tiles["proj_rows"])
    r3 = lambda y: y.reshape(b, s, -1)
    a_p = _attn_prompt(lam4, sg, r3(qb), r3(kb), r3(vb), tiles["attn_q"])
    c_p = _conv_prompt(r3(u), w_dw, bdw, cg, tiles["conv_rows"])
    y_p = _finish(xp, a_p.reshape(b * s, ATTN_W), c_p.reshape(b * s, CONV_W), w_out_b, nffn,
                  w_query_b, sub_keys_b, eu_b, ev_b, tiles, tiles["proj_rows"])

    rows = tiles["sample_rows"]
    xs = jnp.pad(x_sample.reshape(bd, D_MODEL), ((0, rows - bd), (0, 0)))
    qs, ks, _, vs, _, us = _in_proj(xs, nmix, w_in_b, qg, kg, pmat, rows)
    ks, vs, us = ks[:bd], vs[:bd], us[:bd]
    h3 = lambda y: y.reshape(bd, N_HEADS, HEAD_W)
    a_s = _attn_sample(page_table, lam4, sg, h3(qs[:bd].astype(F32)), h3(ks), h3(vs),
                       cache_k, cache_v, tiles["sample_pages"])
    c_s = _conv_sample(state_conv, us, w_dw, bdw, cg)
    pad = lambda y: jnp.pad(y, ((0, rows - bd), (0, 0)))
    y_s = _finish(xs, pad(a_s.reshape(bd, ATTN_W)), pad(c_s), w_out_b, nffn,
                  w_query_b, sub_keys_b, eu_b, ev_b, tiles, rows)[:bd]

    heads = lambda y, n: y.reshape(n, -1, N_HEADS, HEAD_W)
    conv_prompt = r3(u)[:, s - CONV_STATE:]
    conv_sample = jnp.concatenate([state_conv[:, 1:], us[:, None, :]], axis=1)
    return (y_p.reshape(b, s, D_MODEL), y_s.reshape(bd, 1, D_MODEL),
            heads(k, b), heads(v, b), conv_prompt,
            heads(ks, bd), heads(vs, bd), conv_sample)
```
